```python
import jax, jax.numpy as jnp
from jax import lax
import numpy as np

D_MODEL = 1024
BATCH = 4
SEQ = 8192
DEPTH = 4

GRID_W = 64
CTX_LEN = 256
N_MIXERS = 2
N_A_LAYERS = (DEPTH + 1) // 2
N_B_LAYERS = DEPTH // 2
N_VRES = max(N_B_LAYERS - 1, 0)
D_FF = 2816
HGRN_HEADS = 8
HGRN_DK = D_MODEL // HGRN_HEADS
HGRN_CHUNK = 64
RWKV_HEAD = 64
RWKV_HEADS = D_MODEL // RWKV_HEAD
RWKV_DECAY_LORA = 64
RWKV_AAA_LORA = 64
RWKV_MV_LORA = 32
RWKV_GATE_LORA = 128
RMS_EPS = 1e-6
GN_EPS = 64e-5

kernel_name = 'hgrn2_rwkv7_macaron_prefix_trunk'


def rms_norm(x, g):
    xf = x.astype(jnp.float32)
    y = xf * lax.rsqrt(jnp.mean(xf * xf, axis=-1, keepdims=True) + RMS_EPS)
    return (y * g.astype(jnp.float32)).astype(x.dtype)


def modulate(x, g, shift, scale):
    return rms_norm(x, g) * (1 + scale) + shift


def mod_part(m, sub, k):
    start = (3 * sub + k) * D_MODEL
    return m[..., start:start + D_MODEL]


def swiglu(x, w_in, w_out):
    gate, up = jnp.split(x @ w_in, 2, axis=-1)
    return (jax.nn.silu(gate) * up) @ w_out


def to_heads(a, n_heads):
    b_, t_, d_ = a.shape
    return a.reshape(b_, t_, n_heads, d_ // n_heads).transpose(0, 2, 1, 3)


def from_heads(a):
    b_, h_, t_, dh = a.shape
    return a.transpose(0, 2, 1, 3).reshape(b_, t_, h_ * dh)


def head_rms(o):
    of = o.astype(jnp.float32)
    return from_heads(of * lax.rsqrt(jnp.mean(of * of, axis=-1, keepdims=True) + RMS_EPS))


def hgrn_gates(z, lb):
    zf = z.astype(jnp.float32)
    key = (1 - lb) * jax.nn.sigmoid(-zf)
    logf = jnp.log(lb + (1 - lb) * jax.nn.sigmoid(zf))
    return to_heads(key, HGRN_HEADS), to_heads(logf, HGRN_HEADS)


def gla_chunk_scan(q, k, v, logf, s0):
    b_, h_, t_, _ = k.shape
    dv = v.shape[-1]
    n = t_ // HGRN_CHUNK

    def chunks(a):
        a = a.astype(jnp.float32).reshape(b_, h_, n, HGRN_CHUNK, a.shape[-1])
        return jnp.moveaxis(a, 2, 0)

    with_out = q is not None
    mask = jnp.tril(jnp.ones((HGRN_CHUNK, HGRN_CHUNK), bool))[:, :, None]
    xs = (chunks(k), chunks(v), chunks(logf)) + ((chunks(q),) if with_out else ())

    def step(s, inp):
        kc, vc, gc = inp[:3]
        bcum = jnp.cumsum(gc, axis=2)
        btot = bcum[:, :, -1:, :]
        s_new = (jnp.exp(btot)[:, :, 0, :, None] * s
                 + jnp.einsum('bhck,bhcv->bhkv', kc * jnp.exp(btot - bcum), vc))
        if not with_out:
            return s_new, None
        qc = inp[3]
        o_inter = jnp.einsum('bhck,bhkv->bhcv', qc * jnp.exp(bcum), s)
        diff = bcum[:, :, :, None, :] - bcum[:, :, None, :, :]
        dec = jnp.where(mask, jnp.exp(jnp.where(mask, diff, 0.0)), 0.0)
        att = jnp.einsum('bhik,bhijk,bhjk->bhij', qc, dec, kc)
        return s_new, o_inter + jnp.einsum('bhij,bhjv->bhiv', att, vc)

    s_fin, o = lax.scan(step, s0, xs)
    if with_out:
        o = jnp.moveaxis(o, 0, 2).reshape(b_, h_, t_, dv)
    return o, s_fin


def gla_dir(q, k, v, logf, s0, reverse):
    if reverse:
        flip = lambda a: None if a is None else jnp.flip(a, axis=2)
        o, s = gla_chunk_scan(flip(q), flip(k), flip(v), flip(logf), s0)
        return flip(o), s
    return gla_chunk_scan(q, k, v, logf, s0)


def hgrn2_mixer(u, uc, w_in, lb, norm_g, w_out, ctx_out):
    d = D_MODEL
    p_lat = u @ w_in
    p_ctx = uc @ (w_in if ctx_out else w_in[:, :3 * d])

    def state_side(p):
        v = to_heads(p[..., :d].astype(jnp.float32), HGRN_HEADS)
        gates = [hgrn_gates(p[..., (1 + n) * d:(2 + n) * d], lb[n]) for n in range(2)]
        return v, gates

    def out_side(p):
        return to_heads(jax.nn.silu(p[..., 3 * d:4 * d]), HGRN_HEADS), p[..., 4 * d:]

    v_l, gates_l = state_side(p_lat)
    v_c, gates_c = state_side(p_ctx)
    q_l, g_l = out_side(p_lat)
    q_c, g_c = out_side(p_ctx) if ctx_out else (None, None)
    s0 = jnp.zeros((u.shape[0], HGRN_HEADS, HGRN_DK, HGRN_DK), jnp.float32)
    o_l = 0.0
    o_c = 0.0
    for n, reverse in enumerate((False, True)):
        (k_c, f_c), (k_l, f_l) = gates_c[n], gates_l[n]
        oc_n, s_ctx = gla_dir(q_c, k_c, v_c, f_c, s0, reverse)
        ol_n, _ = gla_dir(q_l, k_l, v_l, f_l, s_ctx, reverse)
        o_l = o_l + ol_n
        if ctx_out:
            o_c = o_c + oc_n

    def readout(o, g, dtype):
        return ((head_rms(o) * norm_g).astype(dtype) * jax.nn.silu(g)) @ w_out

    y = readout(o_l, g_l, u.dtype)
    yc = readout(o_c, g_c, uc.dtype) if ctx_out else None
    return y, yc


def token_shift_grid(x):
    b_, t_, d_ = x.shape
    rows = t_ // GRID_W
    xq = x.reshape(b_, rows, GRID_W, 4, d_ // 4)
    left = jnp.pad(xq[:, :, :-1, 0], ((0, 0), (0, 0), (1, 0), (0, 0)))
    right = jnp.pad(xq[:, :, 1:, 1], ((0, 0), (0, 0), (0, 1), (0, 0)))
    up = jnp.pad(xq[:, :-1, :, 2], ((0, 0), (1, 0), (0, 0), (0, 0)))
    down = jnp.pad(xq[:, 1:, :, 3], ((0, 0), (0, 1), (0, 0), (0, 0)))
    return jnp.stack([left, right, up, down], axis=3).reshape(b_, t_, d_)


def token_shift_seq(x):
    b_, t_, d_ = x.shape
    xh = x.reshape(b_, t_, 2, d_ // 2)
    prev = jnp.pad(xh[:, :-1, 0], ((0, 0), (1, 0), (0, 0)))
    nxt = jnp.pad(xh[:, 1:, 1], ((0, 0), (0, 1), (0, 0)))
    return jnp.stack([prev, nxt], axis=2).reshape(b_, t_, d_)


def l2norm_heads(a):
    b_, t_, d_ = a.shape
    ah = a.astype(jnp.float32).reshape(b_, t_, RWKV_HEADS, RWKV_HEAD)
    ah = ah * lax.rsqrt(jnp.maximum(jnp.sum(ah * ah, axis=-1, keepdims=True), 1e-24))
    return ah.reshape(b_, t_, d_)


def rwkv_features(x, xs, lp, vres, v_first, need_out):
    mu, w_rkv, w0, w1, w2, a0, a1, a2, g1, g2, k_k, k_a = lp
    xx = xs - x
    xw, xk, xv, xa = (x + xx * mu[n] for n in (1, 2, 3, 4))
    k = xk @ w_rkv[1]
    v = xv @ w_rkv[2]
    if vres is None:
        v_first = v
    else:
        v0, v1, v2 = vres
        v = v + (v_first - v) * jax.nn.sigmoid(v0 + (xv @ v1) @ v2)
    kk = l2norm_heads(k * k_k)
    dirs = []
    for n in range(2):
        wpre = (w0[n] + jnp.tanh(xw @ w1[n]) @ w2[n]).astype(jnp.float32)
        decay = jnp.exp(-jnp.exp(-jax.nn.softplus(-wpre) - 0.5))
        a = jax.nn.sigmoid((a0[n] + (xa @ a1[n]) @ a2[n]).astype(jnp.float32))
        k_n = k.astype(jnp.float32) * (1 + (a - 1) * k_a)
        dirs.append((decay, k_n, kk * a))
    r = g = None
    if need_out:
        r = (x + xx * mu[0]) @ w_rkv[0]
        g = jax.nn.sigmoid((x + xx * mu[5]) @ g1) @ g2
    return r, g, v, kk, dirs, v_first


def rwkv_scan(decay, k, v, kk, b, r, s0, reverse):
    b_, t_, d_ = k.shape

    def tm(a):
        return jnp.moveaxis(a.astype(jnp.float32).reshape(b_, t_, RWKV_HEADS, RWKV_HEAD), 1, 0)

    with_out = r is not None
    xs = (tm(decay), tm(k), tm(v), tm(kk), tm(b)) + ((tm(r),) if with_out else ())

    def step(s, inp):
        dec, kt, vt, kkt, bt = inp[:5]
        sa = jnp.einsum('bhvk,bhk->bhv', s, kkt)
        s = s * dec[:, :, None, :] - sa[..., None] * bt[:, :, None, :] + vt[..., None] * kt[:, :, None, :]
        y = jnp.einsum('bhvk,bhk->bhv', s, inp[5]) if with_out else None
        return s, y

    s_fin, y = lax.scan(step, s0, xs, reverse=reverse)
    if with_out:
        y = jnp.moveaxis(y, 0, 1).reshape(b_, t_, d_)
    return y, s_fin


def rwkv_readout(y, r, k_f, k_b, v, g, r_k, lnx_w, lnx_b, w_o, dtype):
    b_, t_, d_ = y.shape
    shp = (b_, t_, RWKV_HEADS, RWKV_HEAD)
    yh = y.reshape(shp)
    mean = jnp.mean(yh, axis=-1, keepdims=True)
    var = jnp.mean(jnp.square(yh - mean), axis=-1, keepdims=True)
    yn = ((yh - mean) * lax.rsqrt(var + GN_EPS)).reshape(b_, t_, d_) * lnx_w + lnx_b
    kb = (0.5 * (k_f + k_b)).reshape(shp)
    bonus = jnp.sum(r.astype(jnp.float32).reshape(shp) * kb * r_k, axis=-1, keepdims=True) * v.astype(jnp.float32).reshape(shp)
    return ((yn + bonus.reshape(b_, t_, d_)).astype(dtype) * g) @ w_o


def rwkv7_mixer(u, uc, lp, r_k, lnx_w, lnx_b, w_o, vres, v_first, v_first_c, ctx_out):
    r, g, v, kk, dirs, v_first = rwkv_features(u, token_shift_grid(u), lp, vres, v_first, True)
    rc, gc, vc, kkc, dirs_c, v_first_c = rwkv_features(uc, token_shift_seq(uc), lp, vres, v_first_c, ctx_out)
    s0 = jnp.zeros((u.shape[0], RWKV_HEADS, RWKV_HEAD, RWKV_HEAD), jnp.float32)
    y = 0.0
    yc = 0.0
    for n, reverse in enumerate((False, True)):
        dec_c, k_c, b_c = dirs_c[n]
        yc_n, s_ctx = rwkv_scan(dec_c, k_c, vc, kkc, b_c, rc, s0, reverse)
        dec_l, k_l, b_l = dirs[n]
        y_n, _ = rwkv_scan(dec_l, k_l, v, kk, b_l, r, s_ctx, reverse)
        y = y + y_n
        if ctx_out:
            yc = yc + yc_n
    out = rwkv_readout(y, r, dirs[0][1], dirs[1][1], v, g, r_k, lnx_w, lnx_b, w_o, u.dtype)
    out_c = rwkv_readout(yc, rc, dirs_c[0][1], dirs_c[1][1], vc, gc, r_k, lnx_w, lnx_b, w_o, uc.dtype) if ctx_out else None
    return out, out_c, v_first, v_first_c


def setup_inputs(seed: int = 0) -> dict:
    key = jax.random.key(seed)
    ks = iter(jax.random.split(key, 48))
    d = D_MODEL
    na, nb, nv = N_A_LAYERS, N_B_LAYERS, N_VRES

    def nrm(shape, s):
        return jax.random.normal(next(ks), shape, jnp.float32) * s

    return {
        'x': nrm((BATCH, SEQ, d), 1.0),
        'c': nrm((BATCH, d), 1.0),
        'ctx': nrm((BATCH, CTX_LEN, d), 1.0),
        'c_ctx': nrm((d,), 1.0),
        'norm_w': 1.0 + nrm((DEPTH, 3, d), 0.02),
        'mod_w': nrm((DEPTH, d, 9 * d), 0.5 * d ** -0.5),
        'mod_b': nrm((DEPTH, 9 * d), 0.02),
        'ffn_w_in': nrm((DEPTH, 2, d, 2 * D_FF), d ** -0.5),
        'ffn_w_out': nrm((DEPTH, 2, D_FF, d), D_FF ** -0.5),
        'hgrn_w_in': nrm((na, d, 5 * d), d ** -0.5),
        'hgrn_lb': nrm((na, 2, d), 0.5),
        'hgrn_norm_w': 1.0 + nrm((na, d), 0.02),
        'hgrn_w_out': nrm((na, d, d), d ** -0.5),
        'rwkv_mu': jax.random.uniform(next(ks), (nb, 6, d), jnp.float32),
        'rwkv_w_rkv': nrm((nb, 3, d, d), d ** -0.5),
        'rwkv_w_o': nrm((nb, d, d), d ** -0.5),
        'rwkv_w0': -3.0 + nrm((nb, 2, d), 1.5),
        'rwkv_w1': nrm((nb, 2, d, RWKV_DECAY_LORA), d ** -0.5),
        'rwkv_w2': nrm((nb, 2, RWKV_DECAY_LORA, d), 0.5 * RWKV_DECAY_LORA ** -0.5),
        'rwkv_a0': nrm((nb, 2, d), 0.5),
        'rwkv_a1': nrm((nb, 2, d, RWKV_AAA_LORA), d ** -0.5),
        'rwkv_a2': nrm((nb, 2, RWKV_AAA_LORA, d), 0.5 * RWKV_AAA_LORA ** -0.5),
        'rwkv_v0': nrm((nv, d), 0.5),
        'rwkv_v1': nrm((nv, d, RWKV_MV_LORA), d ** -0.5),
        'rwkv_v2': nrm((nv, RWKV_MV_LORA, d), 0.5 * RWKV_MV_LORA ** -0.5),
        'rwkv_g1': nrm((nb, d, RWKV_GATE_LORA), d ** -0.5),
        'rwkv_g2': nrm((nb, RWKV_GATE_LORA, d), RWKV_GATE_LORA ** -0.5),
        'rwkv_k_k': 1.0 + nrm((nb, d), 0.1),
        'rwkv_k_a': 1.0 + nrm((nb, d), 0.1),
        'rwkv_r_k': nrm((nb, RWKV_HEADS, RWKV_HEAD), 0.1),
        'rwkv_lnx_w': 1.0 + nrm((nb, d), 0.02),
        'rwkv_lnx_b': nrm((nb, d), 0.02),
        'final_norm_w': 1.0 + nrm((d,), 0.02),
    }


def reference(x, c, ctx, c_ctx, norm_w, mod_w, mod_b, ffn_w_in, ffn_w_out,
              hgrn_w_in, hgrn_lb, hgrn_norm_w, hgrn_w_out,
              rwkv_mu, rwkv_w_rkv, rwkv_w_o, rwkv_w0, rwkv_w1, rwkv_w2,
              rwkv_a0, rwkv_a1, rwkv_a2, rwkv_v0, rwkv_v1, rwkv_v2,
              rwkv_g1, rwkv_g2, rwkv_k_k, rwkv_k_a, rwkv_r_k, rwkv_lnx_w, rwkv_lnx_b,
              final_norm_w):
    p = jax.nn.softmax(hgrn_lb.astype(jnp.float32), axis=0)
    lower_bounds = jnp.cumsum(p, axis=0) - p[0]
    s_lat = jax.nn.silu(c)[:, None, :]
    s_ctx = jax.nn.silu(c_ctx)
    h, hc = x, ctx
    v_first = v_first_c = None
    for i in range(DEPTH):
        last = i == DEPTH - 1
        j = i // N_MIXERS
        m = s_lat @ mod_w[i] + mod_b[i]
        n_cols = (5 if last else 9) * D_MODEL
        mc = s_ctx @ mod_w[i][:, :n_cols] + mod_b[i][:n_cols]
        h = h + 0.5 * mod_part(m, 0, 2) * swiglu(modulate(h, norm_w[i, 0], mod_part(m, 0, 0), mod_part(m, 0, 1)), ffn_w_in[i, 0], ffn_w_out[i, 0])
        hc = hc + 0.5 * mod_part(mc, 0, 2) * swiglu(modulate(hc, norm_w[i, 0], mod_part(mc, 0, 0), mod_part(mc, 0, 1)), ffn_w_in[i, 0], ffn_w_out[i, 0])
        u = modulate(h, norm_w[i, 1], mod_part(m, 1, 0), mod_part(m, 1, 1))
        uc = modulate(hc, norm_w[i, 1], mod_part(mc, 1, 0), mod_part(mc, 1, 1))
        if i % N_MIXERS == 0:
            y, yc = hgrn2_mixer(u, uc, hgrn_w_in[j], lower_bounds[j], hgrn_norm_w[j], hgrn_w_out[j], not last)
        else:
            lp = (rwkv_mu[j], rwkv_w_rkv[j], rwkv_w0[j], rwkv_w1[j], rwkv_w2[j], rwkv_a0[j], rwkv_a1[j], rwkv_a2[j],
                  rwkv_g1[j], rwkv_g2[j], rwkv_k_k[j], rwkv_k_a[j])
            vres = None if j == 0 else (rwkv_v0[j - 1], rwkv_v1[j - 1], rwkv_v2[j - 1])
            y, yc, v_first, v_first_c = rwkv7_mixer(u, uc, lp, rwkv_r_k[j], rwkv_lnx_w[j], rwkv_lnx_b[j], rwkv_w_o[j],
                                                    vres, v_first, v_first_c, not last)
        h = h + mod_part(m, 1, 2) * y
        h = h + 0.5 * mod_part(m, 2, 2) * swiglu(modulate(h, norm_w[i, 2], mod_part(m, 2, 0), mod_part(m, 2, 1)), ffn_w_in[i, 1], ffn_w_out[i, 1])
        if not last:
            hc = hc + mod_part(mc, 1, 2) * yc
            hc = hc + 0.5 * mod_part(mc, 2, 2) * swiglu(modulate(hc, norm_w[i, 2], mod_part(mc, 2, 0), mod_part(mc, 2, 1)), ffn_w_in[i, 1], ffn_w_out[i, 1])
    return rms_norm(h, final_norm_w)
```

```python
import functools
import math

import jax
import jax.numpy as jnp
from jax import lax
from jax.experimental import pallas as pl
from jax.experimental.pallas import tpu as pltpu

F32 = jnp.float32
BF16 = jnp.bfloat16

D_MODEL = 1024
DEPTH = 4
N_MIXERS = 2
GRID_W = 64
D_FF = 2816
HGRN_HEADS = 8
HGRN_DK = D_MODEL // HGRN_HEADS
RWKV_HEAD = 64
RWKV_HEADS = D_MODEL // RWKV_HEAD
RMS_EPS = 1e-6
GN_EPS = 64e-5

LANES = 128
VMEM_LIMIT = 56 * 1024 * 1024
FFN_ROWS = 512
PROJ_ROWS = 256
FFN_COLS = D_FF // 2
GLA_CHUNK = 64
GLA_SUB = 16
GLA_BLOCK = 512
RSCAN_STEPS = 32


def _cparams(sem):
    return pltpu.CompilerParams(dimension_semantics=sem, vmem_limit_bytes=VMEM_LIMIT)


def _const_spec(shape):
    nd = len(shape)
    return pl.BlockSpec(shape, lambda *_: (0,) * nd, pipeline_mode=pl.Buffered(1))


def _row_spec(rows):
    return pl.BlockSpec((None, rows, D_MODEL), lambda b, j: (b, j, 0))


def _mod_spec(idx):
    return pl.BlockSpec((None, None, 1, D_MODEL), lambda b, j: (b, idx, 0, 0))


def _dot(a, b):
    return jnp.dot(a, b, preferred_element_type=F32)


def _dot_nt(a, b):
    return lax.dot_general(a, b, (((1,), (1,)), ((), ())), preferred_element_type=F32)


def _dot_tn(a, b):
    return lax.dot_general(a, b, (((0,), (0,)), ((), ())), preferred_element_type=F32)


def _sigmoid(x):
    return 1.0 / (1.0 + jnp.exp(-x))


def _silu(x):
    return x * _sigmoid(x)


def _rms(x, g):
    ms = jnp.mean(x * x, axis=-1, keepdims=True)
    return x * lax.rsqrt(ms + RMS_EPS) * g


def _rms_mod(x, g, shift, scale):
    return _rms(x, g) * (1.0 + scale) + shift


def _split3(x):
    hi = x.astype(BF16)
    r1 = x - hi.astype(F32)
    mid = r1.astype(BF16)
    lo = (r1 - mid.astype(F32)).astype(BF16)
    return hi, mid, lo


def _dot_exact_lhs(m_bf16, x):
    hi, mid, lo = _split3(x)
    return _dot(m_bf16, hi) + _dot(m_bf16, mid) + _dot(m_bf16, lo)


def _seg64_sum(x):
    cols = []
    lane = lax.broadcasted_iota(jnp.int32, (1, LANES), 1)
    low = lane < RWKV_HEAD
    for i in range(x.shape[-1] // LANES):
        xt = x[:, i * LANES:(i + 1) * LANES]
        s_lo = jnp.sum(jnp.where(low, xt, 0.0), axis=-1, keepdims=True)
        s_hi = jnp.sum(jnp.where(low, 0.0, xt), axis=-1, keepdims=True)
        cols.append(jnp.where(low, s_lo, s_hi))
    return jnp.concatenate(cols, axis=-1)


def _mod_body(c_ref, w_ref, b_ref, o_ref):
    s = _silu(c_ref[...])
    o_ref[...] = jnp.dot(s, w_ref[...], preferred_element_type=F32,
                         precision=lax.Precision.HIGHEST) + b_ref[...]


def _mod_call(c_rows, mod_w, mod_b):
    depth, d, n = mod_w.shape
    rows = c_rows.shape[0]
    return pl.pallas_call(
        _mod_body,
        grid=(depth, n // d),
        in_specs=[
            pl.BlockSpec((rows, d), lambda l, j: (0, 0)),
            pl.BlockSpec((None, d, d), lambda l, j: (l, 0, j)),
            pl.BlockSpec((None, 1, d), lambda l, j: (l, 0, j)),
        ],
        out_specs=pl.BlockSpec((None, rows, d), lambda l, j: (l, 0, j)),
        out_shape=jax.ShapeDtypeStruct((depth, rows, n), F32),
        compiler_params=_cparams(("arbitrary", "arbitrary")),
        name="mod",
    )(c_rows, mod_w, mod_b.reshape(depth, 1, n))


def _ffn_body(h_ref, sh_ref, sc_ref, gt_ref, g_ref, win_ref, wout_ref, fw_ref, o_ref, *, final):
    x = h_ref[...]
    xn = _rms_mod(x, g_ref[...], sh_ref[...], sc_ref[...]).astype(BF16)
    acc = None
    for c in range(D_FF // FFN_COLS):
        lo = c * FFN_COLS
        gate = _dot(xn, win_ref[:, lo:lo + FFN_COLS])
        up = _dot(xn, win_ref[:, D_FF + lo:D_FF + lo + FFN_COLS])
        act = (_silu(gate) * up).astype(BF16)
        part = _dot(act, wout_ref[lo:lo + FFN_COLS, :])
        acc = part if acc is None else acc + part
    out = x + 0.5 * gt_ref[...] * acc
    if final:
        out = _rms(out, fw_ref[...])
    o_ref[...] = out


def _ffn_call(h, mtab, sub, norm_g, w_in, w_out, final_w, final):
    b, t, d = h.shape
    rows = min(FFN_ROWS, t)
    return pl.pallas_call(
        functools.partial(_ffn_body, final=final),
        grid=(b, t // rows),
        in_specs=[
            _row_spec(rows),
            _mod_spec(3 * sub), _mod_spec(3 * sub + 1), _mod_spec(3 * sub + 2),
            _const_spec((1, d)),
            _const_spec((d, 2 * D_FF)),
            _const_spec((D_FF, d)),
            _const_spec((1, d)),
        ],
        out_specs=_row_spec(rows),
        out_shape=jax.ShapeDtypeStruct(h.shape, F32),
        compiler_params=_cparams(("parallel", "parallel")),
        name="ffn",
    )(h, mtab, mtab, mtab, norm_g.reshape(1, d), w_in, w_out, final_w.reshape(1, d))


def _chunk_tri(rows, reverse):
    r = lax.broadcasted_iota(jnp.int32, (rows, rows), 0)
    c = lax.broadcasted_iota(jnp.int32, (rows, rows), 1)
    same = (r // GLA_CHUNK) == (c // GLA_CHUNK)
    tri = (c >= r) if reverse else (c <= r)
    return jnp.where(same & tri, 1.0, 0.0).astype(BF16)


def _hproj_body(h_ref, sh_ref, sc_ref, g_ref, w_ref, lb_ref,
                v_ref, q_ref, gs_ref, kf_ref, bf_ref, kb_ref, bb_ref):
    d = D_MODEL
    u = _rms_mod(h_ref[...], g_ref[...], sh_ref[...], sc_ref[...]).astype(BF16)
    rows = u.shape[0]
    v_ref[...] = _dot(u, w_ref[:, 0:d])
    q_ref[...] = _silu(_dot(u, w_ref[:, 3 * d:4 * d]))
    gs_ref[...] = _silu(_dot(u, w_ref[:, 4 * d:5 * d]))
    for n, (k_ref, b_ref) in enumerate(((kf_ref, bf_ref), (kb_ref, bb_ref))):
        z = _dot(u, w_ref[:, (1 + n) * d:(2 + n) * d])
        lb = lb_ref[n:n + 1, :]
        k_ref[...] = (1.0 - lb) * _sigmoid(-z)
        logf = jnp.log(lb + (1.0 - lb) * _sigmoid(z))
        b_ref[...] = _dot_exact_lhs(_chunk_tri(rows, n == 1), logf)


def _hproj_call(h, mtab, norm_g, w_in, lb):
    b, t, d = h.shape
    rows = min(PROJ_ROWS, t)
    out = jax.ShapeDtypeStruct(h.shape, F32)
    return pl.pallas_call(
        _hproj_body,
        grid=(b, t // rows),
        in_specs=[
            _row_spec(rows), _mod_spec(3), _mod_spec(4),
            _const_spec((1, d)), _const_spec((d, 5 * d)), _const_spec((2, d)),
        ],
        out_specs=[_row_spec(rows)] * 7,
        out_shape=[out] * 7,
        compiler_params=_cparams(("parallel", "parallel")),
        name="hproj",
    )(h, mtab, mtab, norm_g.reshape(1, d), w_in, lb)


def _hscan_body(q_ref, k_ref, v_ref, b_ref, s0_ref, o_ref, sfin_ref, st_ref, *, reverse, nchunk):
    c_, s_ = GLA_CHUNK, GLA_SUB
    nsub = c_ // s_
    dk = HGRN_DK

    @pl.when(pl.program_id(2) == 0)
    def _():
        st_ref[...] = s0_ref[...]

    row = lax.broadcasted_iota(jnp.int32, (c_, c_), 0)
    col = lax.broadcasted_iota(jnp.int32, (c_, c_), 1)
    same_sub = (row // s_) == (col // s_)
    col_in_sub = col % s_
    sub_row = lax.broadcasted_iota(jnp.int32, (1, s_, 1), 1)
    row1 = lax.broadcasted_iota(jnp.int32, (c_, 1), 0)

    def chunk(ci, carry):
        cidx = (nchunk - 1 - ci) if reverse else ci
        off = pl.multiple_of(cidx * c_, c_)
        q = q_ref[pl.ds(off, c_), :]
        k = k_ref[pl.ds(off, c_), :]
        v = v_ref[pl.ds(off, c_), :]
        bc = b_ref[pl.ds(off, c_), :]
        st = st_ref[...]

        b3 = bc.reshape(nsub, s_, dk)
        zero = jnp.zeros((1, 1, dk), F32)
        if reverse:
            bref3 = jnp.concatenate([b3[1:, 0:1, :], zero], axis=0)
        else:
            bref3 = jnp.concatenate([zero, b3[:-1, s_ - 1:s_, :]], axis=0)
        q3 = q.reshape(nsub, s_, dk)
        k3 = k.reshape(nsub, s_, dk)
        qs = (q3 * jnp.exp(b3 - bref3)).reshape(c_, dk).astype(BF16)

        o = _dot_nt((q * jnp.exp(bc)).astype(BF16), st.astype(BF16))

        att_rows = []
        for i in range(nsub):
            if (i == nsub - 1) if reverse else (i == 0):
                att_rows.append(jnp.zeros((s_, c_), F32))
                continue
            if reverse:
                ref_row = bc[(i + 1) * s_:(i + 1) * s_ + 1, :]
                valid = row1 >= (i + 1) * s_
            else:
                ref_row = bc[i * s_ - 1:i * s_, :]
                valid = row1 < i * s_
            kd = jnp.where(valid, k * jnp.exp(jnp.where(valid, ref_row - bc, 0.0)), 0.0)
            att_rows.append(_dot_nt(qs[i * s_:(i + 1) * s_, :], kd.astype(BF16)))
        att = jnp.concatenate(att_rows, axis=0)

        for jj in range(s_):
            keep = (sub_row <= jj) if reverse else (sub_row >= jj)
            diff = jnp.where(keep, b3 - b3[:, jj:jj + 1, :], 0.0)
            prod = jnp.where(keep, q3 * jnp.exp(diff) * k3[:, jj:jj + 1, :], 0.0)
            sc = jnp.sum(prod, axis=-1, keepdims=True).reshape(c_, 1)
            att = att + jnp.where(same_sub & (col_in_sub == jj), sc, 0.0)

        o = o + _dot(att.astype(BF16), v.astype(BF16))
        o_ref[pl.ds(off, c_), :] = o

        btot = bc[0:1, :] if reverse else bc[c_ - 1:c_, :]
        kh = (k * jnp.exp(btot - bc)).astype(BF16)
        st_ref[...] = st * jnp.exp(btot) + _dot_tn(v.astype(BF16), kh)
        return carry

    lax.fori_loop(0, nchunk, chunk, 0)
    sfin_ref[...] = st_ref[...]


def _hscan_call(q, k, v, bc, s0, reverse):
    b, t, d = q.shape
    tb = min(GLA_BLOCK, t)
    nblk = t // tb
    dk = HGRN_DK

    def tmap(bi, hi, j):
        return (bi, (nblk - 1 - j) if reverse else j, hi)

    seq = pl.BlockSpec((None, tb, dk), tmap)
    st = pl.BlockSpec((None, None, dk, dk), lambda bi, hi, j: (bi, hi, 0, 0))
    return pl.pallas_call(
        functools.partial(_hscan_body, reverse=reverse, nchunk=tb // GLA_CHUNK),
        grid=(b, HGRN_HEADS, nblk),
        in_specs=[seq, seq, seq, seq, st],
        out_specs=[seq, st],
        out_shape=[jax.ShapeDtypeStruct(q.shape, F32),
                   jax.ShapeDtypeStruct((b, HGRN_HEADS, dk, dk), F32)],
        scratch_shapes=[pltpu.VMEM((dk, dk), F32)],
        compiler_params=_cparams(("parallel", "parallel", "arbitrary")),
        name="hscan_bwd" if reverse else "hscan_fwd",
    )(q, k, v, bc, s0)


def _hread_body(of_ref, ob_ref, gs_ref, h_ref, gt_ref, ng_ref, w_ref, o_ref):
    o = of_ref[...] + ob_ref[...]
    cols = []
    for hd in range(HGRN_HEADS):
        oh = o[:, hd * HGRN_DK:(hd + 1) * HGRN_DK]
        cols.append(oh * lax.rsqrt(jnp.mean(oh * oh, axis=-1, keepdims=True) + RMS_EPS))
    on = jnp.concatenate(cols, axis=-1) * ng_ref[...]
    y = _dot((on * gs_ref[...]).astype(BF16), w_ref[...])
    o_ref[...] = h_ref[...] + gt_ref[...] * y


def _hread_call(o_f, o_b, gs, h, mtab, norm_g, w_out):
    b, t, d = h.shape
    rows = min(PROJ_ROWS, t)
    return pl.pallas_call(
        _hread_body,
        grid=(b, t // rows),
        in_specs=[_row_spec(rows)] * 4 + [_mod_spec(5), _const_spec((1, d)), _const_spec((d, d))],
        out_specs=_row_spec(rows),
        out_shape=jax.ShapeDtypeStruct(h.shape, F32),
        compiler_params=_cparams(("parallel", "parallel")),
        name="hread",
    )(o_f, o_b, gs, h, mtab, norm_g.reshape(1, d), w_out)


def _rfeat_body(*refs, grid_shift, vres):
    it = iter(refs)
    h_ref = next(it)
    hp_ref = next(it) if grid_shift else None
    hn_ref = next(it) if grid_shift else None
    sh_ref, sc_ref, g_ref, mu_ref, wrkv_ref = (next(it) for _ in range(5))
    w0_ref, w1_ref, w2_ref, a0_ref, a1_ref, a2_ref = (next(it) for _ in range(6))
    g1_ref, g2_ref, kk_ref_, ka_ref = (next(it) for _ in range(4))
    if vres:
        v0_ref, v1_ref, v2_ref, vf_ref = (next(it) for _ in range(4))
    (r_out, g_out, v_out, kk_out, df_out, kf_out, bf_out, db_out, kb_out, bb_out) = (next(it) for _ in range(10))

    d = D_MODEL
    g, sh, sc = g_ref[...], sh_ref[...], sc_ref[...]
    u = _rms_mod(h_ref[...], g, sh, sc)
    rows = u.shape[0]
    rid = lax.broadcasted_iota(jnp.int32, (rows, 1), 0)
    if grid_shift:
        j = pl.program_id(1)
        last = pl.num_programs(1) - 1
        up_rows = jnp.where(j > 0, _rms_mod(hp_ref[...], g, sh, sc), 0.0)
        dn_rows = jnp.where(j < last, _rms_mod(hn_ref[...], g, sh, sc), 0.0)
        q4 = d // 4
        col = rid % GRID_W
        left = jnp.where(col != 0, pltpu.roll(u[:, 0:q4], 1, 0), 0.0)
        right = jnp.where(col != GRID_W - 1, pltpu.roll(u[:, q4:2 * q4], rows - 1, 0), 0.0)
        up = jnp.concatenate([up_rows[:, 2 * q4:3 * q4], u[:rows - GRID_W, 2 * q4:3 * q4]], axis=0)
        down = jnp.concatenate([u[GRID_W:, 3 * q4:], dn_rows[:, 3 * q4:]], axis=0)
        xs = jnp.concatenate([left, right, up, down], axis=-1)
    else:
        h2 = d // 2
        prev = jnp.where(rid != 0, pltpu.roll(u[:, 0:h2], 1, 0), 0.0)
        nxt = jnp.where(rid != rows - 1, pltpu.roll(u[:, h2:], rows - 1, 0), 0.0)
        xs = jnp.concatenate([prev, nxt], axis=-1)
    xx = xs - u

    def mix(n):
        return (u + xx * mu_ref[n:n + 1, :]).astype(BF16)

    xv = mix(3)
    k = _dot(mix(2), wrkv_ref[1])
    v = _dot(xv, wrkv_ref[2])
    if vres:
        lora = _dot(_dot(xv, v1_ref[...]).astype(BF16), v2_ref[...])
        v = v + (vf_ref[...] - v) * _sigmoid(v0_ref[...] + lora)
    kk = k * kk_ref_[...]
    kk = kk * lax.rsqrt(jnp.maximum(_seg64_sum(kk * kk), 1e-24))
    r_out[...] = _dot(mix(0), wrkv_ref[0])
    g_out[...] = _dot(_sigmoid(_dot(mix(5), g1_ref[...])).astype(BF16), g2_ref[...])
    v_out[...] = v
    kk_out[...] = kk
    xw = mix(1)
    xa = mix(4)
    ka = ka_ref[...]
    scale = math.exp(-0.5)
    for n, (d_out, k_out, b_out) in enumerate(((df_out, kf_out, bf_out), (db_out, kb_out, bb_out))):
        wpre = w0_ref[n:n + 1, :] + _dot(jnp.tanh(_dot(xw, w1_ref[n])).astype(BF16), w2_ref[n])
        d_out[...] = jnp.exp(-(_sigmoid(wpre) * scale))
        a = _sigmoid(a0_ref[n:n + 1, :] + _dot(_dot(xa, a1_ref[n]).astype(BF16), a2_ref[n]))
        k_out[...] = k * (1.0 + (a - 1.0) * ka)
        b_out[...] = kk * a


def _rfeat_call(h, mtab, norm_g, lp, vres, v_first, grid_shift):
    b, t, d = h.shape
    rows = min(PROJ_ROWS, t) if grid_shift else t
    mu, w_rkv, w0, w1, w2, a0, a1, a2, g1, g2, k_k, k_a = lp
    nrow_blocks = t // GRID_W
    per = rows // GRID_W
    in_specs = [_row_spec(rows)]
    args = [h]
    if grid_shift:
        in_specs += [
            pl.BlockSpec((None, GRID_W, d), lambda bi, j: (bi, jnp.maximum(j * per - 1, 0), 0)),
            pl.BlockSpec((None, GRID_W, d), lambda bi, j: (bi, jnp.minimum((j + 1) * per, nrow_blocks - 1), 0)),
        ]
        args += [h, h]
    params = [norm_g.reshape(1, d), mu, w_rkv, w0, w1, w2, a0, a1, a2, g1, g2,
              k_k.reshape(1, d), k_a.reshape(1, d)]
    in_specs += [_mod_spec(3), _mod_spec(4)] + [_const_spec(p.shape) for p in params]
    args += [mtab, mtab] + params
    if vres is not None:
        v0, v1, v2 = vres
        extra = [v0.reshape(1, d), v1, v2]
        in_specs += [_const_spec(p.shape) for p in extra] + [_row_spec(rows)]
        args += extra + [v_first]
    out = jax.ShapeDtypeStruct(h.shape, F32)
    return pl.pallas_call(
        functools.partial(_rfeat_body, grid_shift=grid_shift, vres=vres is not None),
        grid=(b, t // rows),
        in_specs=in_specs,
        out_specs=[_row_spec(rows)] * 10,
        out_shape=[out] * 10,
        compiler_params=_cparams(("parallel", "parallel")),
        name="rfeat_lat" if grid_shift else "rfeat_ctx",
    )(*args)


def _rscan_body(dec_ref, kn_ref, v_ref, kk_ref, bn_ref, r_ref, y_ref, s_ref):
    hd = RWKV_HEAD

    @pl.when(pl.program_id(1) == 0)
    def _():
        s_ref[...] = jnp.zeros_like(s_ref)

    def step(t, carry):
        dec = dec_ref[t]
        kn = kn_ref[t]
        vv = v_ref[t]
        kk = kk_ref[t]
        bn = bn_ref[t]
        r = r_ref[t]
        dr = dec * r
        sa = jnp.zeros((hd, LANES), F32)
        y1 = jnp.zeros((hd, LANES), F32)
        for k in range(hd):
            sk = s_ref[k]
            sa = sa + sk * kk[k:k + 1, :]
            y1 = y1 + sk * dr[k:k + 1, :]
        br = jnp.sum(bn * r, axis=0, keepdims=True)
        kr = jnp.sum(kn * r, axis=0, keepdims=True)
        y_ref[t] = y1 - sa * br + vv * kr
        for k in range(hd):
            s_ref[k] = s_ref[k] * dec[k:k + 1, :] - sa * bn[k:k + 1, :] + vv * kn[k:k + 1, :]
        return carry

    lax.fori_loop(0, dec_ref.shape[0], step, 0)


def _rscan_call(dec, kn, v, kk, bn, r):
    t, hd, p = dec.shape
    spec = pl.BlockSpec((RSCAN_STEPS, hd, LANES), lambda j, i: (i, 0, j))
    return pl.pallas_call(
        _rscan_body,
        grid=(p // LANES, t // RSCAN_STEPS),
        in_specs=[spec] * 6,
        out_specs=spec,
        out_shape=jax.ShapeDtypeStruct(dec.shape, F32),
        scratch_shapes=[pltpu.VMEM((hd, hd, LANES), F32)],
        compiler_params=_cparams(("parallel", "arbitrary")),
        name="rscan",
    )(dec, kn, v, kk, bn, r)


def _rread_body(yf_ref, yb_ref, r_ref, kf_ref, kb_ref, v_ref, g_ref, h_ref, gt_ref,
                rk_ref, lw_ref, lb_ref, w_ref, o_ref):
    y = yf_ref[...] + yb_ref[...]
    inv = 1.0 / RWKV_HEAD
    mean = _seg64_sum(y) * inv
    yc = y - mean
    var = _seg64_sum(yc * yc) * inv
    yn = yc * lax.rsqrt(var + GN_EPS) * lw_ref[...] + lb_ref[...]
    kb = 0.5 * (kf_ref[...] + kb_ref[...])
    bonus = _seg64_sum(r_ref[...] * kb * rk_ref[...]) * v_ref[...]
    out = _dot(((yn + bonus) * g_ref[...]).astype(BF16), w_ref[...])
    o_ref[...] = h_ref[...] + gt_ref[...] * out


def _rread_call(y_f, y_b, r, k_f, k_b, v, g, h, mtab, r_k, lnx_w, lnx_b, w_o):
    b, t, d = h.shape
    rows = min(PROJ_ROWS, t)
    return pl.pallas_call(
        _rread_body,
        grid=(b, t // rows),
        in_specs=[_row_spec(rows)] * 8 + [_mod_spec(5)] + [_const_spec((1, d))] * 3 + [_const_spec((d, d))],
        out_specs=_row_spec(rows),
        out_shape=jax.ShapeDtypeStruct(h.shape, F32),
        compiler_params=_cparams(("parallel", "parallel")),
        name="rread",
    )(y_f, y_b, r, k_f, k_b, v, g, h, mtab, r_k.reshape(1, d), lnx_w.reshape(1, d), lnx_b.reshape(1, d), w_o)


def _to_lanes(fwd_c, fwd_l, bwd_c, bwd_l, lanes):
    b = fwd_l.shape[0]
    fwd = jnp.concatenate([fwd_c, fwd_l], axis=1)
    bwd = jnp.concatenate([jnp.flip(bwd_c, axis=1), jnp.flip(bwd_l, axis=1)], axis=1)
    both = jnp.stack([fwd, bwd], axis=0)
    tt = both.shape[2]
    both = both.reshape(2, b, tt, RWKV_HEADS, RWKV_HEAD).transpose(2, 4, 0, 1, 3)
    both = both.reshape(tt, RWKV_HEAD, 2 * b * RWKV_HEADS)
    return jnp.pad(both, ((0, 0), (0, 0), (0, lanes - both.shape[-1])))


def _from_lanes(y, b, tc):
    tt = y.shape[0]
    y = y[:, :, :2 * b * RWKV_HEADS].reshape(tt, RWKV_HEAD, 2, b, RWKV_HEADS)
    y = y.transpose(2, 3, 0, 4, 1).reshape(2, b, tt, D_MODEL)
    yf, yb = y[0], y[1]
    return (yf[:, :tc], jnp.flip(yb[:, :tc], axis=1)), (yf[:, tc:], jnp.flip(yb[:, tc:], axis=1))


def kernel(x, c, ctx, c_ctx, norm_w, mod_w, mod_b, ffn_w_in, ffn_w_out, hgrn_w_in, hgrn_lb, hgrn_norm_w, hgrn_w_out, rwkv_mu, rwkv_w_rkv, rwkv_w_o, rwkv_w0, rwkv_w1, rwkv_w2, rwkv_a0, rwkv_a1, rwkv_a2, rwkv_v0, rwkv_v1, rwkv_v2, rwkv_g1, rwkv_g2, rwkv_k_k, rwkv_k_a, rwkv_r_k, rwkv_lnx_w, rwkv_lnx_b, final_norm_w):
    b, t, d = x.shape
    tc = ctx.shape[1]
    depth = norm_w.shape[0]

    p = jax.nn.softmax(hgrn_lb.astype(F32), axis=0)
    lower_bounds = jnp.cumsum(p, axis=0) - p[0]

    rows = -(-(b + 1) // 8) * 8
    c_rows = jnp.zeros((rows, d), F32).at[:b].set(c).at[b].set(c_ctx)
    m_all = _mod_call(c_rows, mod_w, mod_b)

    lanes = -(-(2 * b * RWKV_HEADS) // LANES) * LANES
    h, hc = x, ctx
    v_first = v_first_c = None
    zero_state = jnp.zeros((b, HGRN_HEADS, HGRN_DK, HGRN_DK), F32)
    for i in range(depth):
        last = i == depth - 1
        j = i // N_MIXERS
        mt = m_all[i, :b].reshape(b, 9, 1, d)
        mtc = jnp.broadcast_to(m_all[i, b].reshape(1, 9, 1, d), (b, 9, 1, d))
        w_in = [ffn_w_in[i, s].astype(BF16) for s in range(2)]
        w_out = [ffn_w_out[i, s].astype(BF16) for s in range(2)]

        h = _ffn_call(h, mt, 0, norm_w[i, 0], w_in[0], w_out[0], final_norm_w, False)
        hc = _ffn_call(hc, mtc, 0, norm_w[i, 0], w_in[0], w_out[0], final_norm_w, False)

        if i % N_MIXERS == 0:
            hw_in = hgrn_w_in[j].astype(BF16)
            hw_out = hgrn_w_out[j].astype(BF16)
            v_c, q_c, gs_c, kf_c, bf_c, kb_c, bb_c = _hproj_call(hc, mtc, norm_w[i, 1], hw_in, lower_bounds[j])
            v_l, q_l, gs_l, kf_l, bf_l, kb_l, bb_l = _hproj_call(h, mt, norm_w[i, 1], hw_in, lower_bounds[j])
            ocf, s_f = _hscan_call(q_c, kf_c, v_c, bf_c, zero_state, False)
            ocb, s_b = _hscan_call(q_c, kb_c, v_c, bb_c, zero_state, True)
            olf, _ = _hscan_call(q_l, kf_l, v_l, bf_l, s_f, False)
            olb, _ = _hscan_call(q_l, kb_l, v_l, bb_l, s_b, True)
            h = _hread_call(olf, olb, gs_l, h, mt, hgrn_norm_w[j], hw_out)
            if not last:
                hc = _hread_call(ocf, ocb, gs_c, hc, mtc, hgrn_norm_w[j], hw_out)
        else:
            lp = (rwkv_mu[j], rwkv_w_rkv[j].astype(BF16), rwkv_w0[j], rwkv_w1[j].astype(BF16),
                  rwkv_w2[j].astype(BF16), rwkv_a0[j], rwkv_a1[j].astype(BF16), rwkv_a2[j].astype(BF16),
                  rwkv_g1[j].astype(BF16), rwkv_g2[j].astype(BF16), rwkv_k_k[j], rwkv_k_a[j])
            vres = None if j == 0 else (rwkv_v0[j - 1], rwkv_v1[j - 1].astype(BF16), rwkv_v2[j - 1].astype(BF16))
            fc = _rfeat_call(hc, mtc, norm_w[i, 1], lp, vres, v_first_c, False)
            fl = _rfeat_call(h, mt, norm_w[i, 1], lp, vres, v_first, True)
            r_c, g_c, vv_c, kk_c, df_c, kf_c, bf_c, db_c, kb_c, bb_c = fc
            r_l, g_l, vv_l, kk_l, df_l, kf_l, bf_l, db_l, kb_l, bb_l = fl
            if vres is None:
                v_first, v_first_c = vv_l, vv_c
            y = _rscan_call(
                _to_lanes(df_c, df_l, db_c, db_l, lanes),
                _to_lanes(kf_c, kf_l, kb_c, kb_l, lanes),
                _to_lanes(vv_c, vv_l, vv_c, vv_l, lanes),
                _to_lanes(kk_c, kk_l, kk_c, kk_l, lanes),
                _to_lanes(bf_c, bf_l, bb_c, bb_l, lanes),
                _to_lanes(r_c, r_l, r_c, r_l, lanes))
            (ycf, ycb), (ylf, ylb) = _from_lanes(y, b, tc)
            wo = rwkv_w_o[j].astype(BF16)
            rk = rwkv_r_k[j].reshape(d)
            h = _rread_call(ylf, ylb, r_l, kf_l, kb_l, vv_l, g_l, h, mt, rk, rwkv_lnx_w[j], rwkv_lnx_b[j], wo)
            if not last:
                hc = _rread_call(ycf, ycb, r_c, kf_c, kb_c, vv_c, g_c, hc, mtc, rk, rwkv_lnx_w[j], rwkv_lnx_b[j], wo)

        h = _ffn_call(h, mt, 2, norm_w[i, 2], w_in[1], w_out[1], final_norm_w, last)
        if not last:
            hc = _ffn_call(hc, mtc, 2, norm_w[i, 2], w_in[1], w_out[1], final_norm_w, False)
    return h
```

```python
import functools
import math

import jax
import jax.numpy as jnp
from jax import lax
from jax.experimental import pallas as pl
from jax.experimental.pallas import tpu as pltpu

F32 = jnp.float32
BF16 = jnp.bfloat16

D_MODEL = 1024
DEPTH = 4
N_MIXERS = 2
GRID_W = 64
D_FF = 2816
HGRN_HEADS = 8
HGRN_DK = D_MODEL // HGRN_HEADS
RWKV_HEAD = 64
RWKV_HEADS = D_MODEL // RWKV_HEAD
RMS_EPS = 1e-6
GN_EPS = 64e-5

LANES = 128
VMEM_LIMIT = 56 * 1024 * 1024
FFN_ROWS = 512
PROJ_ROWS = 256
FFN_COLS = D_FF // 2
GLA_CHUNK = 64
GLA_SUB = 16
GLA_BLOCK = 512
RSCAN_STEPS = 16


def _cparams(sem):
    return pltpu.CompilerParams(dimension_semantics=sem, vmem_limit_bytes=VMEM_LIMIT)


def _const_spec(shape):
    nd = len(shape)
    return pl.BlockSpec(shape, lambda *_: (0,) * nd, pipeline_mode=pl.Buffered(1))


def _row_spec(rows):
    return pl.BlockSpec((None, rows, D_MODEL), lambda b, j: (b, j, 0))


def _mod_spec(idx):
    return pl.BlockSpec((None, None, 1, D_MODEL), lambda b, j: (b, idx, 0, 0))


def _dot(a, b):
    return jnp.dot(a, b, preferred_element_type=F32)


def _dot_nt(a, b):
    return lax.dot_general(a, b, (((1,), (1,)), ((), ())), preferred_element_type=F32)


def _dot_tn(a, b):
    return lax.dot_general(a, b, (((0,), (0,)), ((), ())), preferred_element_type=F32)


def _sigmoid(x):
    return 1.0 / (1.0 + jnp.exp(-x))


def _silu(x):
    return x * _sigmoid(x)


def _rms(x, g):
    ms = jnp.mean(x * x, axis=-1, keepdims=True)
    return x * lax.rsqrt(ms + RMS_EPS) * g


def _rms_mod(x, g, shift, scale):
    return _rms(x, g) * (1.0 + scale) + shift


def _split3(x):
    hi = x.astype(BF16)
    r1 = x - hi.astype(F32)
    mid = r1.astype(BF16)
    lo = (r1 - mid.astype(F32)).astype(BF16)
    return hi, mid, lo


def _dot_exact_lhs(m_bf16, x):
    hi, mid, lo = _split3(x)
    return _dot(m_bf16, hi) + _dot(m_bf16, mid) + _dot(m_bf16, lo)


def _seg64_sum(x):
    cols = []
    lane = lax.broadcasted_iota(jnp.int32, (1, LANES), 1)
    low = lane < RWKV_HEAD
    for i in range(x.shape[-1] // LANES):
        xt = x[:, i * LANES:(i + 1) * LANES]
        s_lo = jnp.sum(jnp.where(low, xt, 0.0), axis=-1, keepdims=True)
        s_hi = jnp.sum(jnp.where(low, 0.0, xt), axis=-1, keepdims=True)
        cols.append(jnp.where(low, s_lo, s_hi))
    return jnp.concatenate(cols, axis=-1)


def _mod_body(c_ref, w_ref, b_ref, o_ref):
    s = _silu(c_ref[...])
    o_ref[...] = jnp.dot(s, w_ref[...], preferred_element_type=F32,
                         precision=lax.Precision.HIGHEST) + b_ref[...]


def _mod_call(c_rows, mod_w, mod_b):
    depth, d, n = mod_w.shape
    rows = c_rows.shape[0]
    return pl.pallas_call(
        _mod_body,
        grid=(depth, n // d),
        in_specs=[
            pl.BlockSpec((rows, d), lambda l, j: (0, 0)),
            pl.BlockSpec((None, d, d), lambda l, j: (l, 0, j)),
            pl.BlockSpec((None, 1, d), lambda l, j: (l, 0, j)),
        ],
        out_specs=pl.BlockSpec((None, rows, d), lambda l, j: (l, 0, j)),
        out_shape=jax.ShapeDtypeStruct((depth, rows, n), F32),
        compiler_params=_cparams(("arbitrary", "arbitrary")),
        name="mod",
    )(c_rows, mod_w, mod_b.reshape(depth, 1, n))


def _ffn_body(h_ref, sh_ref, sc_ref, gt_ref, g_ref, win_ref, wout_ref, fw_ref, o_ref, *, final):
    x = h_ref[...]
    xn = _rms_mod(x, g_ref[...], sh_ref[...], sc_ref[...]).astype(BF16)
    acc = None
    for c in range(D_FF // FFN_COLS):
        lo = c * FFN_COLS
        gate = _dot(xn, win_ref[:, lo:lo + FFN_COLS])
        up = _dot(xn, win_ref[:, D_FF + lo:D_FF + lo + FFN_COLS])
        act = (_silu(gate) * up).astype(BF16)
        part = _dot(act, wout_ref[lo:lo + FFN_COLS, :])
        acc = part if acc is None else acc + part
    out = x + 0.5 * gt_ref[...] * acc
    if final:
        out = _rms(out, fw_ref[...])
    o_ref[...] = out


def _ffn_call(h, mtab, sub, norm_g, w_in, w_out, final_w, final):
    b, t, d = h.shape
    rows = min(FFN_ROWS, t)
    return pl.pallas_call(
        functools.partial(_ffn_body, final=final),
        grid=(b, t // rows),
        in_specs=[
            _row_spec(rows),
            _mod_spec(3 * sub), _mod_spec(3 * sub + 1), _mod_spec(3 * sub + 2),
            _const_spec((1, d)),
            _const_spec((d, 2 * D_FF)),
            _const_spec((D_FF, d)),
            _const_spec((1, d)),
        ],
        out_specs=_row_spec(rows),
        out_shape=jax.ShapeDtypeStruct(h.shape, F32),
        compiler_params=_cparams(("parallel", "parallel")),
        name="ffn",
    )(h, mtab, mtab, mtab, norm_g.reshape(1, d), w_in, w_out, final_w.reshape(1, d))


def _chunk_tri(rows, reverse):
    r = lax.broadcasted_iota(jnp.int32, (rows, rows), 0)
    c = lax.broadcasted_iota(jnp.int32, (rows, rows), 1)
    same = (r // GLA_CHUNK) == (c // GLA_CHUNK)
    tri = (c >= r) if reverse else (c <= r)
    return jnp.where(same & tri, 1.0, 0.0).astype(BF16)


def _hproj_body(h_ref, sh_ref, sc_ref, g_ref, w_ref, lb_ref,
                v_ref, q_ref, gs_ref, kf_ref, bf_ref, kb_ref, bb_ref):
    d = D_MODEL
    u = _rms_mod(h_ref[...], g_ref[...], sh_ref[...], sc_ref[...]).astype(BF16)
    rows = u.shape[0]
    v_ref[...] = _dot(u, w_ref[:, 0:d])
    q_ref[...] = _silu(_dot(u, w_ref[:, 3 * d:4 * d]))
    gs_ref[...] = _silu(_dot(u, w_ref[:, 4 * d:5 * d]))
    for n, (k_ref, b_ref) in enumerate(((kf_ref, bf_ref), (kb_ref, bb_ref))):
        z = _dot(u, w_ref[:, (1 + n) * d:(2 + n) * d])
        lb = lb_ref[n:n + 1, :]
        k_ref[...] = (1.0 - lb) * _sigmoid(-z)
        logf = jnp.log(lb + (1.0 - lb) * _sigmoid(z))
        b_ref[...] = _dot_exact_lhs(_chunk_tri(rows, n == 1), logf)


def _hproj_call(h, mtab, norm_g, w_in, lb):
    b, t, d = h.shape
    rows = min(PROJ_ROWS, t)
    out = jax.ShapeDtypeStruct(h.shape, F32)
    return pl.pallas_call(
        _hproj_body,
        grid=(b, t // rows),
        in_specs=[
            _row_spec(rows), _mod_spec(3), _mod_spec(4),
            _const_spec((1, d)), _const_spec((d, 5 * d)), _const_spec((2, d)),
        ],
        out_specs=[_row_spec(rows)] * 7,
        out_shape=[out] * 7,
        compiler_params=_cparams(("parallel", "parallel")),
        name="hproj",
    )(h, mtab, mtab, norm_g.reshape(1, d), w_in, lb)


def _gla_chunk(q, k, v, bc, st, reverse):
    c_, s_ = GLA_CHUNK, GLA_SUB
    nsub = c_ // s_
    dk = HGRN_DK
    row = lax.broadcasted_iota(jnp.int32, (c_, c_), 0)
    col = lax.broadcasted_iota(jnp.int32, (c_, c_), 1)
    same_sub = (row // s_) == (col // s_)
    col_in_sub = col % s_
    sub_row = lax.broadcasted_iota(jnp.int32, (1, s_, 1), 1)
    row1 = lax.broadcasted_iota(jnp.int32, (c_, 1), 0)

    b3 = bc.reshape(nsub, s_, dk)
    zero = jnp.zeros((1, 1, dk), F32)
    if reverse:
        bref3 = jnp.concatenate([b3[1:, 0:1, :], zero], axis=0)
    else:
        bref3 = jnp.concatenate([zero, b3[:-1, s_ - 1:s_, :]], axis=0)
    q3 = q.reshape(nsub, s_, dk)
    k3 = k.reshape(nsub, s_, dk)
    qs = (q3 * jnp.exp(b3 - bref3)).reshape(c_, dk).astype(BF16)

    o = _dot_nt((q * jnp.exp(bc)).astype(BF16), st.astype(BF16))

    att_rows = []
    for i in range(nsub):
        if (i == nsub - 1) if reverse else (i == 0):
            att_rows.append(jnp.zeros((s_, c_), F32))
            continue
        if reverse:
            ref_row = bc[(i + 1) * s_:(i + 1) * s_ + 1, :]
            valid = row1 >= (i + 1) * s_
        else:
            ref_row = bc[i * s_ - 1:i * s_, :]
            valid = row1 < i * s_
        kd = jnp.where(valid, k * jnp.exp(jnp.where(valid, ref_row - bc, 0.0)), 0.0)
        att_rows.append(_dot_nt(qs[i * s_:(i + 1) * s_, :], kd.astype(BF16)))
    att = jnp.concatenate(att_rows, axis=0)

    for jj in range(s_):
        keep = (sub_row <= jj) if reverse else (sub_row >= jj)
        prod = jnp.where(keep, q3 * jnp.exp(b3 - b3[:, jj:jj + 1, :]) * k3[:, jj:jj + 1, :], 0.0)
        sc = jnp.sum(prod, axis=-1, keepdims=True).reshape(c_, 1)
        att = att + jnp.where(same_sub & (col_in_sub == jj), sc, 0.0)

    o = o + _dot(att.astype(BF16), v.astype(BF16))
    btot = bc[0:1, :] if reverse else bc[c_ - 1:c_, :]
    kh = (k * jnp.exp(btot - bc)).astype(BF16)
    return o, st * jnp.exp(btot) + _dot_tn(v.astype(BF16), kh)


def _hscan_body(qf_ref, kf_ref, vf_ref, bf_ref, qb_ref, kb_ref, vb_ref, bb_ref, s0f_ref, s0b_ref,
                of_ref, ob_ref, sff_ref, sfb_ref, stf_ref, stb_ref, *, nchunk):
    c_ = GLA_CHUNK

    @pl.when(pl.program_id(2) == 0)
    def _():
        stf_ref[...] = s0f_ref[...]
        stb_ref[...] = s0b_ref[...]

    def chunk(ci, carry):
        off = pl.multiple_of(ci * c_, c_)
        o, st = _gla_chunk(qf_ref[pl.ds(off, c_), :], kf_ref[pl.ds(off, c_), :], vf_ref[pl.ds(off, c_), :],
                           bf_ref[pl.ds(off, c_), :], stf_ref[...], False)
        of_ref[pl.ds(off, c_), :] = o
        stf_ref[...] = st
        offb = pl.multiple_of((nchunk - 1 - ci) * c_, c_)
        o, st = _gla_chunk(qb_ref[pl.ds(offb, c_), :], kb_ref[pl.ds(offb, c_), :], vb_ref[pl.ds(offb, c_), :],
                           bb_ref[pl.ds(offb, c_), :], stb_ref[...], True)
        ob_ref[pl.ds(offb, c_), :] = o
        stb_ref[...] = st
        return carry

    lax.fori_loop(0, nchunk, chunk, 0)
    sff_ref[...] = stf_ref[...]
    sfb_ref[...] = stb_ref[...]


def _hscan_call(q, v, k_f, bc_f, k_b, bc_b, s0_f, s0_b):
    b, t, d = q.shape
    tb = min(GLA_BLOCK, t)
    nblk = t // tb
    dk = HGRN_DK
    fwd = pl.BlockSpec((None, tb, dk), lambda bi, hi, j: (bi, j, hi))
    bwd = pl.BlockSpec((None, tb, dk), lambda bi, hi, j: (bi, nblk - 1 - j, hi))
    st = pl.BlockSpec((None, None, dk, dk), lambda bi, hi, j: (bi, hi, 0, 0))
    seq_shape = jax.ShapeDtypeStruct(q.shape, F32)
    st_shape = jax.ShapeDtypeStruct((b, HGRN_HEADS, dk, dk), F32)
    return pl.pallas_call(
        functools.partial(_hscan_body, nchunk=tb // GLA_CHUNK),
        grid=(b, HGRN_HEADS, nblk),
        in_specs=[fwd] * 4 + [bwd] * 4 + [st, st],
        out_specs=[fwd, bwd, st, st],
        out_shape=[seq_shape, seq_shape, st_shape, st_shape],
        scratch_shapes=[pltpu.VMEM((dk, dk), F32), pltpu.VMEM((dk, dk), F32)],
        compiler_params=_cparams(("parallel", "parallel", "arbitrary")),
        name="hscan",
    )(q, k_f, v, bc_f, q, k_b, v, bc_b, s0_f, s0_b)


def _hread_body(of_ref, ob_ref, gs_ref, h_ref, gt_ref, ng_ref, w_ref, o_ref):
    o = of_ref[...] + ob_ref[...]
    cols = []
    for hd in range(HGRN_HEADS):
        oh = o[:, hd * HGRN_DK:(hd + 1) * HGRN_DK]
        cols.append(oh * lax.rsqrt(jnp.mean(oh * oh, axis=-1, keepdims=True) + RMS_EPS))
    on = jnp.concatenate(cols, axis=-1) * ng_ref[...]
    y = _dot((on * gs_ref[...]).astype(BF16), w_ref[...])
    o_ref[...] = h_ref[...] + gt_ref[...] * y


def _hread_call(o_f, o_b, gs, h, mtab, norm_g, w_out):
    b, t, d = h.shape
    rows = min(PROJ_ROWS, t)
    return pl.pallas_call(
        _hread_body,
        grid=(b, t // rows),
        in_specs=[_row_spec(rows)] * 4 + [_mod_spec(5), _const_spec((1, d)), _const_spec((d, d))],
        out_specs=_row_spec(rows),
        out_shape=jax.ShapeDtypeStruct(h.shape, F32),
        compiler_params=_cparams(("parallel", "parallel")),
        name="hread",
    )(o_f, o_b, gs, h, mtab, norm_g.reshape(1, d), w_out)


def _rfeat_body(*refs, grid_shift, vres):
    it = iter(refs)
    h_ref = next(it)
    hp_ref = next(it) if grid_shift else None
    hn_ref = next(it) if grid_shift else None
    sh_ref, sc_ref, g_ref, mu_ref, wrkv_ref = (next(it) for _ in range(5))
    w0_ref, w1_ref, w2_ref, a0_ref, a1_ref, a2_ref = (next(it) for _ in range(6))
    g1_ref, g2_ref, kk_ref_, ka_ref = (next(it) for _ in range(4))
    if vres:
        v0_ref, v1_ref, v2_ref, vf_ref = (next(it) for _ in range(4))
    (r_out, g_out, v_out, kk_out, df_out, kf_out, bf_out, db_out, kb_out, bb_out) = (next(it) for _ in range(10))

    d = D_MODEL
    g, sh, sc = g_ref[...], sh_ref[...], sc_ref[...]
    u = _rms_mod(h_ref[...], g, sh, sc)
    rows = u.shape[0]
    rid = lax.broadcasted_iota(jnp.int32, (rows, 1), 0)
    if grid_shift:
        j = pl.program_id(1)
        last = pl.num_programs(1) - 1
        up_rows = jnp.where(j > 0, _rms_mod(hp_ref[...], g, sh, sc), 0.0)
        dn_rows = jnp.where(j < last, _rms_mod(hn_ref[...], g, sh, sc), 0.0)
        q4 = d // 4
        col = rid % GRID_W
        left = jnp.where(col != 0, pltpu.roll(u[:, 0:q4], 1, 0), 0.0)
        right = jnp.where(col != GRID_W - 1, pltpu.roll(u[:, q4:2 * q4], rows - 1, 0), 0.0)
        up = jnp.concatenate([up_rows[:, 2 * q4:3 * q4], u[:rows - GRID_W, 2 * q4:3 * q4]], axis=0)
        down = jnp.concatenate([u[GRID_W:, 3 * q4:], dn_rows[:, 3 * q4:]], axis=0)
        xs = jnp.concatenate([left, right, up, down], axis=-1)
    else:
        h2 = d // 2
        prev = jnp.where(rid != 0, pltpu.roll(u[:, 0:h2], 1, 0), 0.0)
        nxt = jnp.where(rid != rows - 1, pltpu.roll(u[:, h2:], rows - 1, 0), 0.0)
        xs = jnp.concatenate([prev, nxt], axis=-1)
    xx = xs - u

    def mix(n):
        return (u + xx * mu_ref[n:n + 1, :]).astype(BF16)

    xv = mix(3)
    k = _dot(mix(2), wrkv_ref[1])
    v = _dot(xv, wrkv_ref[2])
    if vres:
        lora = _dot(_dot(xv, v1_ref[...]).astype(BF16), v2_ref[...])
        v = v + (vf_ref[...] - v) * _sigmoid(v0_ref[...] + lora)
    kk = k * kk_ref_[...]
    kk = kk * lax.rsqrt(jnp.maximum(_seg64_sum(kk * kk), 1e-24))
    r_out[...] = _dot(mix(0), wrkv_ref[0])
    g_out[...] = _dot(_sigmoid(_dot(mix(5), g1_ref[...])).astype(BF16), g2_ref[...])
    v_out[...] = v
    kk_out[...] = kk
    xw = mix(1)
    xa = mix(4)
    ka = ka_ref[...]
    scale = math.exp(-0.5)
    for n, (d_out, k_out, b_out) in enumerate(((df_out, kf_out, bf_out), (db_out, kb_out, bb_out))):
        wpre = w0_ref[n:n + 1, :] + _dot(jnp.tanh(_dot(xw, w1_ref[n])).astype(BF16), w2_ref[n])
        d_out[...] = jnp.exp(-(_sigmoid(wpre) * scale))
        a = _sigmoid(a0_ref[n:n + 1, :] + _dot(_dot(xa, a1_ref[n]).astype(BF16), a2_ref[n]))
        k_out[...] = k * (1.0 + (a - 1.0) * ka)
        b_out[...] = kk * a


def _rfeat_call(h, mtab, norm_g, lp, vres, v_first, grid_shift):
    b, t, d = h.shape
    rows = min(PROJ_ROWS, t) if grid_shift else t
    mu, w_rkv, w0, w1, w2, a0, a1, a2, g1, g2, k_k, k_a = lp
    nrow_blocks = t // GRID_W
    per = rows // GRID_W
    in_specs = [_row_spec(rows)]
    args = [h]
    if grid_shift:
        in_specs += [
            pl.BlockSpec((None, GRID_W, d), lambda bi, j: (bi, jnp.maximum(j * per - 1, 0), 0)),
            pl.BlockSpec((None, GRID_W, d), lambda bi, j: (bi, jnp.minimum((j + 1) * per, nrow_blocks - 1), 0)),
        ]
        args += [h, h]
    params = [norm_g.reshape(1, d), mu, w_rkv, w0, w1, w2, a0, a1, a2, g1, g2,
              k_k.reshape(1, d), k_a.reshape(1, d)]
    in_specs += [_mod_spec(3), _mod_spec(4)] + [_const_spec(p.shape) for p in params]
    args += [mtab, mtab] + params
    if vres is not None:
        v0, v1, v2 = vres
        extra = [v0.reshape(1, d), v1, v2]
        in_specs += [_const_spec(p.shape) for p in extra] + [_row_spec(rows)]
        args += extra + [v_first]
    out = jax.ShapeDtypeStruct(h.shape, F32)
    return pl.pallas_call(
        functools.partial(_rfeat_body, grid_shift=grid_shift, vres=vres is not None),
        grid=(b, t // rows),
        in_specs=in_specs,
        out_specs=[_row_spec(rows)] * 10,
        out_shape=[out] * 10,
        compiler_params=_cparams(("parallel", "parallel")),
        name="rfeat_lat" if grid_shift else "rfeat_ctx",
    )(*args)


def _rscan_body(*refs):
    fwd, bwd = refs[0:6], refs[6:12]
    s0_ref, yf_ref, yb_ref, sfin_ref, s_ref = refs[12:17]
    hd = RWKV_HEAD
    nsteps = yf_ref.shape[0]

    @pl.when(pl.program_id(0) == 0)
    def _():
        s_ref[...] = s0_ref[...]

    is_fwd = lax.broadcasted_iota(jnp.int32, (1, LANES), 1) < LANES // 2

    def step(t, carry):
        tb = nsteps - 1 - t
        dec, kn, vv, kk, bn, r = (jnp.where(is_fwd, f[t], g[tb]) for f, g in zip(fwd, bwd))
        dr = dec * r
        sa = jnp.zeros((hd, LANES), F32)
        y1 = jnp.zeros((hd, LANES), F32)
        for k in range(hd):
            sk = s_ref[k]
            sa = sa + sk * kk[k:k + 1, :]
            y1 = y1 + sk * dr[k:k + 1, :]
        br = jnp.sum(bn * r, axis=0, keepdims=True)
        kr = jnp.sum(kn * r, axis=0, keepdims=True)
        y = y1 - sa * br + vv * kr
        yf_ref[t] = y
        yb_ref[tb] = y
        for k in range(hd):
            s_ref[k] = s_ref[k] * dec[k:k + 1, :] - sa * bn[k:k + 1, :] + vv * kn[k:k + 1, :]
        return carry

    lax.fori_loop(0, nsteps, step, 0)

    @pl.when(pl.program_id(0) == pl.num_programs(0) - 1)
    def _():
        sfin_ref[...] = s_ref[...]


def _rscan_call(feats, s0):
    t, hd, p = feats[0].shape
    assert p == LANES
    nblk = t // RSCAN_STEPS
    fwd = pl.BlockSpec((RSCAN_STEPS, hd, LANES), lambda i: (i, 0, 0))
    bwd = pl.BlockSpec((RSCAN_STEPS, hd, LANES), lambda i: (nblk - 1 - i, 0, 0))
    st = pl.BlockSpec((hd, hd, LANES), lambda i: (0, 0, 0))
    seq_shape = jax.ShapeDtypeStruct((t, hd, LANES), F32)
    return pl.pallas_call(
        _rscan_body,
        grid=(nblk,),
        in_specs=[fwd] * 6 + [bwd] * 6 + [st],
        out_specs=[fwd, bwd, st],
        out_shape=[seq_shape, seq_shape, jax.ShapeDtypeStruct((hd, hd, LANES), F32)],
        scratch_shapes=[pltpu.VMEM((hd, hd, LANES), F32)],
        compiler_params=_cparams(("arbitrary",)),
        name="rscan",
    )(*feats, *feats, s0)


def _rread_body(yf_ref, yb_ref, r_ref, kf_ref, kb_ref, v_ref, g_ref, h_ref, gt_ref,
                rk_ref, lw_ref, lb_ref, w_ref, o_ref):
    y = yf_ref[...] + yb_ref[...]
    inv = 1.0 / RWKV_HEAD
    mean = _seg64_sum(y) * inv
    yc = y - mean
    var = _seg64_sum(yc * yc) * inv
    yn = yc * lax.rsqrt(var + GN_EPS) * lw_ref[...] + lb_ref[...]
    kb = 0.5 * (kf_ref[...] + kb_ref[...])
    bonus = _seg64_sum(r_ref[...] * kb * rk_ref[...]) * v_ref[...]
    out = _dot(((yn + bonus) * g_ref[...]).astype(BF16), w_ref[...])
    o_ref[...] = h_ref[...] + gt_ref[...] * out


def _rread_call(y_f, y_b, r, k_f, k_b, v, g, h, mtab, r_k, lnx_w, lnx_b, w_o):
    b, t, d = h.shape
    rows = min(PROJ_ROWS, t)
    return pl.pallas_call(
        _rread_body,
        grid=(b, t // rows),
        in_specs=[_row_spec(rows)] * 8 + [_mod_spec(5)] + [_const_spec((1, d))] * 3 + [_const_spec((d, d))],
        out_specs=_row_spec(rows),
        out_shape=jax.ShapeDtypeStruct(h.shape, F32),
        compiler_params=_cparams(("parallel", "parallel")),
        name="rread",
    )(y_f, y_b, r, k_f, k_b, v, g, h, mtab, r_k.reshape(1, d), lnx_w.reshape(1, d), lnx_b.reshape(1, d), w_o)


def _to_lanes(fwd, bwd):
    b, t, _ = fwd.shape
    half = LANES // 2
    assert b * RWKV_HEADS <= half
    both = jnp.stack([fwd, bwd], axis=0).reshape(2, b, t, RWKV_HEADS, RWKV_HEAD)
    both = jnp.pad(both, ((0, 0), (0, half // RWKV_HEADS - b), (0, 0), (0, 0), (0, 0)))
    return both.transpose(2, 4, 0, 1, 3).reshape(t, RWKV_HEAD, LANES)


def _from_lanes(y, direction, b):
    t = y.shape[0]
    half = LANES // 2
    y = y[:, :, direction * half:(direction + 1) * half].reshape(t, RWKV_HEAD, half // RWKV_HEADS, RWKV_HEADS)
    return y[:, :, :b].transpose(2, 0, 3, 1).reshape(b, t, D_MODEL)


def kernel(x, c, ctx, c_ctx, norm_w, mod_w, mod_b, ffn_w_in, ffn_w_out, hgrn_w_in, hgrn_lb, hgrn_norm_w, hgrn_w_out, rwkv_mu, rwkv_w_rkv, rwkv_w_o, rwkv_w0, rwkv_w1, rwkv_w2, rwkv_a0, rwkv_a1, rwkv_a2, rwkv_v0, rwkv_v1, rwkv_v2, rwkv_g1, rwkv_g2, rwkv_k_k, rwkv_k_a, rwkv_r_k, rwkv_lnx_w, rwkv_lnx_b, final_norm_w):
    b, t, d = x.shape
    tc = ctx.shape[1]
    depth = norm_w.shape[0]

    p = jax.nn.softmax(hgrn_lb.astype(F32), axis=0)
    lower_bounds = jnp.cumsum(p, axis=0) - p[0]

    rows = -(-(b + 1) // 8) * 8
    c_rows = jnp.zeros((rows, d), F32).at[:b].set(c).at[b].set(c_ctx)
    m_all = _mod_call(c_rows, mod_w, mod_b)

    h, hc = x, ctx
    v_first = v_first_c = None
    zero_state = jnp.zeros((b, HGRN_HEADS, HGRN_DK, HGRN_DK), F32)
    for i in range(depth):
        last = i == depth - 1
        j = i // N_MIXERS
        mt = m_all[i, :b].reshape(b, 9, 1, d)
        mtc = jnp.broadcast_to(m_all[i, b].reshape(1, 9, 1, d), (b, 9, 1, d))
        w_in = [ffn_w_in[i, s].astype(BF16) for s in range(2)]
        w_out = [ffn_w_out[i, s].astype(BF16) for s in range(2)]

        h = _ffn_call(h, mt, 0, norm_w[i, 0], w_in[0], w_out[0], final_norm_w, False)
        hc = _ffn_call(hc, mtc, 0, norm_w[i, 0], w_in[0], w_out[0], final_norm_w, False)

        if i % N_MIXERS == 0:
            hw_in = hgrn_w_in[j].astype(BF16)
            hw_out = hgrn_w_out[j].astype(BF16)
            v_c, q_c, gs_c, kf_c, bf_c, kb_c, bb_c = _hproj_call(hc, mtc, norm_w[i, 1], hw_in, lower_bounds[j])
            v_l, q_l, gs_l, kf_l, bf_l, kb_l, bb_l = _hproj_call(h, mt, norm_w[i, 1], hw_in, lower_bounds[j])
            ocf, ocb, s_f, s_b = _hscan_call(q_c, v_c, kf_c, bf_c, kb_c, bb_c, zero_state, zero_state)
            olf, olb, _, _ = _hscan_call(q_l, v_l, kf_l, bf_l, kb_l, bb_l, s_f, s_b)
            h = _hread_call(olf, olb, gs_l, h, mt, hgrn_norm_w[j], hw_out)
            if not last:
                hc = _hread_call(ocf, ocb, gs_c, hc, mtc, hgrn_norm_w[j], hw_out)
        else:
            lp = (rwkv_mu[j], rwkv_w_rkv[j].astype(BF16), rwkv_w0[j], rwkv_w1[j].astype(BF16),
                  rwkv_w2[j].astype(BF16), rwkv_a0[j], rwkv_a1[j].astype(BF16), rwkv_a2[j].astype(BF16),
                  rwkv_g1[j].astype(BF16), rwkv_g2[j].astype(BF16), rwkv_k_k[j], rwkv_k_a[j])
            vres = None if j == 0 else (rwkv_v0[j - 1], rwkv_v1[j - 1].astype(BF16), rwkv_v2[j - 1].astype(BF16))
            fc = _rfeat_call(hc, mtc, norm_w[i, 1], lp, vres, v_first_c, False)
            fl = _rfeat_call(h, mt, norm_w[i, 1], lp, vres, v_first, True)
            r_c, g_c, vv_c, kk_c, df_c, kf_c, bf_c, db_c, kb_c, bb_c = fc
            r_l, g_l, vv_l, kk_l, df_l, kf_l, bf_l, db_l, kb_l, bb_l = fl
            if vres is None:
                v_first, v_first_c = vv_l, vv_c
            lanes_c = [_to_lanes(f, g) for f, g in
                       ((df_c, db_c), (kf_c, kb_c), (vv_c, vv_c), (kk_c, kk_c), (bf_c, bb_c), (r_c, r_c))]
            lanes_l = [_to_lanes(f, g) for f, g in
                       ((df_l, db_l), (kf_l, kb_l), (vv_l, vv_l), (kk_l, kk_l), (bf_l, bb_l), (r_l, r_l))]
            ycf, ycb, s_ctx = _rscan_call(lanes_c, jnp.zeros((RWKV_HEAD, RWKV_HEAD, LANES), F32))
            ylf, ylb, _ = _rscan_call(lanes_l, s_ctx)
            ycf, ycb = _from_lanes(ycf, 0, b), _from_lanes(ycb, 1, b)
            ylf, ylb = _from_lanes(ylf, 0, b), _from_lanes(ylb, 1, b)
            wo = rwkv_w_o[j].astype(BF16)
            rk = rwkv_r_k[j].reshape(d)
            h = _rread_call(ylf, ylb, r_l, kf_l, kb_l, vv_l, g_l, h, mt, rk, rwkv_lnx_w[j], rwkv_lnx_b[j], wo)
            if not last:
                hc = _rread_call(ycf, ycb, r_c, kf_c, kb_c, vv_c, g_c, hc, mtc, rk, rwkv_lnx_w[j], rwkv_lnx_b[j], wo)

        h = _ffn_call(h, mt, 2, norm_w[i, 2], w_in[1], w_out[1], final_norm_w, last)
        if not last:
            hc = _ffn_call(hc, mtc, 2, norm_w[i, 2], w_in[1], w_out[1], final_norm_w, False)
    return h
```

```python
import functools
import math

import jax
import jax.numpy as jnp
from jax import lax
from jax.experimental import pallas as pl
from jax.experimental.pallas import tpu as pltpu

F32 = jnp.float32
BF16 = jnp.bfloat16

D_MODEL = 1024
DEPTH = 4
N_MIXERS = 2
GRID_W = 64
D_FF = 2816
HGRN_HEADS = 8
HGRN_DK = D_MODEL // HGRN_HEADS
RWKV_HEAD = 64
RWKV_HEADS = D_MODEL // RWKV_HEAD
RMS_EPS = 1e-6
GN_EPS = 64e-5

LANES = 128
VMEM_LIMIT = 56 * 1024 * 1024
FFN_ROWS = 1024
PROJ_ROWS = 256
FFN_COLS = D_FF // 11
GLA_CHUNK = 64
GLA_SUB = 16
GLA_BLOCK = 512
RSCAN_STEPS = 16


def _cparams(sem):
    return pltpu.CompilerParams(dimension_semantics=sem, vmem_limit_bytes=VMEM_LIMIT)


def _const_spec(shape):
    nd = len(shape)
    return pl.BlockSpec(shape, lambda *_: (0,) * nd, pipeline_mode=pl.Buffered(1))


def _row_spec(rows):
    return pl.BlockSpec((None, rows, D_MODEL), lambda b, j: (b, j, 0))


def _mod_spec(idx):
    return pl.BlockSpec((None, None, 1, D_MODEL), lambda b, j: (b, idx, 0, 0))


def _dot(a, b):
    return jnp.dot(a, b, preferred_element_type=F32)


def _dot_nt(a, b):
    return lax.dot_general(a, b, (((1,), (1,)), ((), ())), preferred_element_type=F32)


def _dot_tn(a, b):
    return lax.dot_general(a, b, (((0,), (0,)), ((), ())), preferred_element_type=F32)


def _sigmoid(x):
    return 1.0 / (1.0 + jnp.exp(-x))


def _silu(x):
    return x * _sigmoid(x)


def _rms(x, g):
    ms = jnp.mean(x * x, axis=-1, keepdims=True)
    return x * lax.rsqrt(ms + RMS_EPS) * g


def _rms_mod(x, g, shift, scale):
    return _rms(x, g) * (1.0 + scale) + shift


def _split3(x):
    hi = x.astype(BF16)
    r1 = x - hi.astype(F32)
    mid = r1.astype(BF16)
    lo = (r1 - mid.astype(F32)).astype(BF16)
    return hi, mid, lo


def _dot_exact_lhs(m_bf16, x):
    hi, mid, lo = _split3(x)
    return _dot(m_bf16, hi) + _dot(m_bf16, mid) + _dot(m_bf16, lo)


def _seg64_sum(x):
    cols = []
    lane = lax.broadcasted_iota(jnp.int32, (1, LANES), 1)
    low = lane < RWKV_HEAD
    for i in range(x.shape[-1] // LANES):
        xt = x[:, i * LANES:(i + 1) * LANES]
        s_lo = jnp.sum(jnp.where(low, xt, 0.0), axis=-1, keepdims=True)
        s_hi = jnp.sum(jnp.where(low, 0.0, xt), axis=-1, keepdims=True)
        cols.append(jnp.where(low, s_lo, s_hi))
    return jnp.concatenate(cols, axis=-1)


def _mod_body(c_ref, w_ref, b_ref, o_ref):
    s = _silu(c_ref[...])
    o_ref[...] = jnp.dot(s, w_ref[...], preferred_element_type=F32,
                         precision=lax.Precision.HIGHEST) + b_ref[...]


def _mod_call(c_rows, mod_w, mod_b):
    depth, d, n = mod_w.shape
    rows = c_rows.shape[0]
    return pl.pallas_call(
        _mod_body,
        grid=(depth, n // d),
        in_specs=[
            pl.BlockSpec((rows, d), lambda l, j: (0, 0)),
            pl.BlockSpec((None, d, d), lambda l, j: (l, 0, j)),
            pl.BlockSpec((None, 1, d), lambda l, j: (l, 0, j)),
        ],
        out_specs=pl.BlockSpec((None, rows, d), lambda l, j: (l, 0, j)),
        out_shape=jax.ShapeDtypeStruct((depth, rows, n), F32),
        compiler_params=_cparams(("arbitrary", "arbitrary")),
        name="mod",
    )(c_rows, mod_w, mod_b.reshape(depth, 1, n))


def _ffn_body(h_ref, sh_ref, sc_ref, gt_ref, g_ref, win_ref, wout_ref, fw_ref, o_ref, *, final):
    x = h_ref[...]
    xn = _rms_mod(x, g_ref[...], sh_ref[...], sc_ref[...]).astype(BF16)
    acc = None
    for c in range(D_FF // FFN_COLS):
        lo = c * FFN_COLS
        gate = _dot(xn, win_ref[:, lo:lo + FFN_COLS])
        up = _dot(xn, win_ref[:, D_FF + lo:D_FF + lo + FFN_COLS])
        act = (_silu(gate) * up).astype(BF16)
        part = _dot(act, wout_ref[lo:lo + FFN_COLS, :])
        acc = part if acc is None else acc + part
    out = x + 0.5 * gt_ref[...] * acc
    if final:
        out = _rms(out, fw_ref[...])
    o_ref[...] = out


def _ffn_call(h, mtab, sub, norm_g, w_in, w_out, final_w, final):
    b, t, d = h.shape
    rows = min(FFN_ROWS, t)
    return pl.pallas_call(
        functools.partial(_ffn_body, final=final),
        grid=(b, t // rows),
        in_specs=[
            _row_spec(rows),
            _mod_spec(3 * sub), _mod_spec(3 * sub + 1), _mod_spec(3 * sub + 2),
            _const_spec((1, d)),
            _const_spec((d, 2 * D_FF)),
            _const_spec((D_FF, d)),
            _const_spec((1, d)),
        ],
        out_specs=_row_spec(rows),
        out_shape=jax.ShapeDtypeStruct(h.shape, F32),
        compiler_params=_cparams(("parallel", "parallel")),
        name="ffn",
    )(h, mtab, mtab, mtab, norm_g.reshape(1, d), w_in, w_out, final_w.reshape(1, d))


def _chunk_tri(rows, reverse):
    r = lax.broadcasted_iota(jnp.int32, (rows, rows), 0)
    c = lax.broadcasted_iota(jnp.int32, (rows, rows), 1)
    same = (r // GLA_CHUNK) == (c // GLA_CHUNK)
    tri = (c >= r) if reverse else (c <= r)
    return jnp.where(same & tri, 1.0, 0.0).astype(BF16)


def _hproj_body(h_ref, sh_ref, sc_ref, g_ref, w_ref, lb_ref,
                v_ref, q_ref, gs_ref, kf_ref, bf_ref, kb_ref, bb_ref):
    d = D_MODEL
    u = _rms_mod(h_ref[...], g_ref[...], sh_ref[...], sc_ref[...]).astype(BF16)
    rows = u.shape[0]
    v_ref[...] = _dot(u, w_ref[:, 0:d])
    q_ref[...] = _silu(_dot(u, w_ref[:, 3 * d:4 * d]))
    gs_ref[...] = _silu(_dot(u, w_ref[:, 4 * d:5 * d]))
    for n, (k_ref, b_ref) in enumerate(((kf_ref, bf_ref), (kb_ref, bb_ref))):
        z = _dot(u, w_ref[:, (1 + n) * d:(2 + n) * d])
        lb = lb_ref[n:n + 1, :]
        k_ref[...] = (1.0 - lb) * _sigmoid(-z)
        logf = jnp.log(lb + (1.0 - lb) * _sigmoid(z))
        b_ref[...] = _dot_exact_lhs(_chunk_tri(rows, n == 1), logf)


def _hproj_call(h, mtab, norm_g, w_in, lb):
    b, t, d = h.shape
    rows = min(PROJ_ROWS, t)
    out = jax.ShapeDtypeStruct(h.shape, F32)
    return pl.pallas_call(
        _hproj_body,
        grid=(b, t // rows),
        in_specs=[
            _row_spec(rows), _mod_spec(3), _mod_spec(4),
            _const_spec((1, d)), _const_spec((d, 5 * d)), _const_spec((2, d)),
        ],
        out_specs=[_row_spec(rows)] * 7,
        out_shape=[out] * 7,
        compiler_params=_cparams(("parallel", "parallel")),
        name="hproj",
    )(h, mtab, mtab, norm_g.reshape(1, d), w_in, lb)


def _gla_chunk(q, k, v, bc, st, reverse):
    c_, s_ = GLA_CHUNK, GLA_SUB
    nsub = c_ // s_
    dk = HGRN_DK
    row = lax.broadcasted_iota(jnp.int32, (c_, c_), 0)
    col = lax.broadcasted_iota(jnp.int32, (c_, c_), 1)
    same_sub = (row // s_) == (col // s_)
    col_in_sub = col % s_
    sub_row = lax.broadcasted_iota(jnp.int32, (1, s_, 1), 1)
    row1 = lax.broadcasted_iota(jnp.int32, (c_, 1), 0)

    b3 = bc.reshape(nsub, s_, dk)
    zero = jnp.zeros((1, 1, dk), F32)
    if reverse:
        bref3 = jnp.concatenate([b3[1:, 0:1, :], zero], axis=0)
    else:
        bref3 = jnp.concatenate([zero, b3[:-1, s_ - 1:s_, :]], axis=0)
    q3 = q.reshape(nsub, s_, dk)
    k3 = k.reshape(nsub, s_, dk)
    qs = (q3 * jnp.exp(b3 - bref3)).reshape(c_, dk).astype(BF16)

    o = _dot_nt((q * jnp.exp(bc)).astype(BF16), st.astype(BF16))

    att_rows = []
    for i in range(nsub):
        if (i == nsub - 1) if reverse else (i == 0):
            att_rows.append(jnp.zeros((s_, c_), F32))
            continue
        if reverse:
            ref_row = bc[(i + 1) * s_:(i + 1) * s_ + 1, :]
            valid = row1 >= (i + 1) * s_
        else:
            ref_row = bc[i * s_ - 1:i * s_, :]
            valid = row1 < i * s_
        kd = jnp.where(valid, k * jnp.exp(jnp.where(valid, ref_row - bc, 0.0)), 0.0)
        att_rows.append(_dot_nt(qs[i * s_:(i + 1) * s_, :], kd.astype(BF16)))
    att = jnp.concatenate(att_rows, axis=0)

    for jj in range(s_):
        keep = (sub_row <= jj) if reverse else (sub_row >= jj)
        prod = jnp.where(keep, q3 * jnp.exp(b3 - b3[:, jj:jj + 1, :]) * k3[:, jj:jj + 1, :], 0.0)
        sc = jnp.sum(prod, axis=-1, keepdims=True).reshape(c_, 1)
        att = att + jnp.where(same_sub & (col_in_sub == jj), sc, 0.0)

    o = o + _dot(att.astype(BF16), v.astype(BF16))
    btot = bc[0:1, :] if reverse else bc[c_ - 1:c_, :]
    kh = (k * jnp.exp(btot - bc)).astype(BF16)
    return o, st * jnp.exp(btot) + _dot_tn(v.astype(BF16), kh)


def _hscan_body(qf_ref, kf_ref, vf_ref, bf_ref, qb_ref, kb_ref, vb_ref, bb_ref, s0f_ref, s0b_ref,
                of_ref, ob_ref, sff_ref, sfb_ref, stf_ref, stb_ref, *, nchunk):
    c_ = GLA_CHUNK

    @pl.when(pl.program_id(2) == 0)
    def _():
        stf_ref[...] = s0f_ref[...]
        stb_ref[...] = s0b_ref[...]

    def chunk(ci, carry):
        off = pl.multiple_of(ci * c_, c_)
        o, st = _gla_chunk(qf_ref[pl.ds(off, c_), :], kf_ref[pl.ds(off, c_), :], vf_ref[pl.ds(off, c_), :],
                           bf_ref[pl.ds(off, c_), :], stf_ref[...], False)
        of_ref[pl.ds(off, c_), :] = o
        stf_ref[...] = st
        offb = pl.multiple_of((nchunk - 1 - ci) * c_, c_)
        o, st = _gla_chunk(qb_ref[pl.ds(offb, c_), :], kb_ref[pl.ds(offb, c_), :], vb_ref[pl.ds(offb, c_), :],
                           bb_ref[pl.ds(offb, c_), :], stb_ref[...], True)
        ob_ref[pl.ds(offb, c_), :] = o
        stb_ref[...] = st
        return carry

    lax.fori_loop(0, nchunk, chunk, 0)
    sff_ref[...] = stf_ref[...]
    sfb_ref[...] = stb_ref[...]


def _hscan_call(q, v, k_f, bc_f, k_b, bc_b, s0_f, s0_b):
    b, t, d = q.shape
    tb = min(GLA_BLOCK, t)
    nblk = t // tb
    dk = HGRN_DK
    fwd = pl.BlockSpec((None, tb, dk), lambda bi, hi, j: (bi, j, hi))
    bwd = pl.BlockSpec((None, tb, dk), lambda bi, hi, j: (bi, nblk - 1 - j, hi))
    st = pl.BlockSpec((None, None, dk, dk), lambda bi, hi, j: (bi, hi, 0, 0))
    seq_shape = jax.ShapeDtypeStruct(q.shape, F32)
    st_shape = jax.ShapeDtypeStruct((b, HGRN_HEADS, dk, dk), F32)
    return pl.pallas_call(
        functools.partial(_hscan_body, nchunk=tb // GLA_CHUNK),
        grid=(b, HGRN_HEADS, nblk),
        in_specs=[fwd] * 4 + [bwd] * 4 + [st, st],
        out_specs=[fwd, bwd, st, st],
        out_shape=[seq_shape, seq_shape, st_shape, st_shape],
        scratch_shapes=[pltpu.VMEM((dk, dk), F32), pltpu.VMEM((dk, dk), F32)],
        compiler_params=_cparams(("parallel", "parallel", "arbitrary")),
        name="hscan",
    )(q, k_f, v, bc_f, q, k_b, v, bc_b, s0_f, s0_b)


def _hread_body(of_ref, ob_ref, gs_ref, h_ref, gt_ref, ng_ref, w_ref, o_ref):
    o = of_ref[...] + ob_ref[...]
    cols = []
    for hd in range(HGRN_HEADS):
        oh = o[:, hd * HGRN_DK:(hd + 1) * HGRN_DK]
        cols.append(oh * lax.rsqrt(jnp.mean(oh * oh, axis=-1, keepdims=True) + RMS_EPS))
    on = jnp.concatenate(cols, axis=-1) * ng_ref[...]
    y = _dot((on * gs_ref[...]).astype(BF16), w_ref[...])
    o_ref[...] = h_ref[...] + gt_ref[...] * y


def _hread_call(o_f, o_b, gs, h, mtab, norm_g, w_out):
    b, t, d = h.shape
    rows = min(PROJ_ROWS, t)
    return pl.pallas_call(
        _hread_body,
        grid=(b, t // rows),
        in_specs=[_row_spec(rows)] * 4 + [_mod_spec(5), _const_spec((1, d)), _const_spec((d, d))],
        out_specs=_row_spec(rows),
        out_shape=jax.ShapeDtypeStruct(h.shape, F32),
        compiler_params=_cparams(("parallel", "parallel")),
        name="hread",
    )(o_f, o_b, gs, h, mtab, norm_g.reshape(1, d), w_out)


def _rfeat_body(*refs, grid_shift, vres):
    it = iter(refs)
    h_ref = next(it)
    hp_ref = next(it) if grid_shift else None
    hn_ref = next(it) if grid_shift else None
    sh_ref, sc_ref, g_ref, mu_ref, wrkv_ref = (next(it) for _ in range(5))
    w0_ref, w1_ref, w2_ref, a0_ref, a1_ref, a2_ref = (next(it) for _ in range(6))
    g1_ref, g2_ref, kk_ref_, ka_ref = (next(it) for _ in range(4))
    if vres:
        v0_ref, v1_ref, v2_ref, vf_ref = (next(it) for _ in range(4))
    (r_out, g_out, v_out, kk_out, df_out, kf_out, bf_out, db_out, kb_out, bb_out) = (next(it) for _ in range(10))

    d = D_MODEL
    g, sh, sc = g_ref[...], sh_ref[...], sc_ref[...]
    u = _rms_mod(h_ref[...], g, sh, sc)
    rows = u.shape[0]
    rid = lax.broadcasted_iota(jnp.int32, (rows, 1), 0)
    if grid_shift:
        j = pl.program_id(1)
        last = pl.num_programs(1) - 1
        up_rows = jnp.where(j > 0, _rms_mod(hp_ref[...], g, sh, sc), 0.0)
        dn_rows = jnp.where(j < last, _rms_mod(hn_ref[...], g, sh, sc), 0.0)
        q4 = d // 4
        col = rid % GRID_W
        left = jnp.where(col != 0, pltpu.roll(u[:, 0:q4], 1, 0), 0.0)
        right = jnp.where(col != GRID_W - 1, pltpu.roll(u[:, q4:2 * q4], rows - 1, 0), 0.0)
        up = jnp.concatenate([up_rows[:, 2 * q4:3 * q4], u[:rows - GRID_W, 2 * q4:3 * q4]], axis=0)
        down = jnp.concatenate([u[GRID_W:, 3 * q4:], dn_rows[:, 3 * q4:]], axis=0)
        xs = jnp.concatenate([left, right, up, down], axis=-1)
    else:
        h2 = d // 2
        prev = jnp.where(rid != 0, pltpu.roll(u[:, 0:h2], 1, 0), 0.0)
        nxt = jnp.where(rid != rows - 1, pltpu.roll(u[:, h2:], rows - 1, 0), 0.0)
        xs = jnp.concatenate([prev, nxt], axis=-1)
    xx = xs - u

    def mix(n):
        return (u + xx * mu_ref[n:n + 1, :]).astype(BF16)

    xv = mix(3)
    k = _dot(mix(2), wrkv_ref[1])
    v = _dot(xv, wrkv_ref[2])
    if vres:
        lora = _dot(_dot(xv, v1_ref[...]).astype(BF16), v2_ref[...])
        v = v + (vf_ref[...] - v) * _sigmoid(v0_ref[...] + lora)
    kk = k * kk_ref_[...]
    kk = kk * lax.rsqrt(jnp.maximum(_seg64_sum(kk * kk), 1e-24))
    r_out[...] = _dot(mix(0), wrkv_ref[0])
    g_out[...] = _dot(_sigmoid(_dot(mix(5), g1_ref[...])).astype(BF16), g2_ref[...])
    v_out[...] = v
    kk_out[...] = kk
    xw = mix(1)
    xa = mix(4)
    ka = ka_ref[...]
    scale = math.exp(-0.5)
    for n, (d_out, k_out, b_out) in enumerate(((df_out, kf_out, bf_out), (db_out, kb_out, bb_out))):
        wpre = w0_ref[n:n + 1, :] + _dot(jnp.tanh(_dot(xw, w1_ref[n])).astype(BF16), w2_ref[n])
        d_out[...] = jnp.exp(-(_sigmoid(wpre) * scale))
        a = _sigmoid(a0_ref[n:n + 1, :] + _dot(_dot(xa, a1_ref[n]).astype(BF16), a2_ref[n]))
        k_out[...] = k * (1.0 + (a - 1.0) * ka)
        b_out[...] = kk * a


def _rfeat_call(h, mtab, norm_g, lp, vres, v_first, grid_shift):
    b, t, d = h.shape
    rows = min(PROJ_ROWS, t) if grid_shift else t
    mu, w_rkv, w0, w1, w2, a0, a1, a2, g1, g2, k_k, k_a = lp
    nrow_blocks = t // GRID_W
    per = rows // GRID_W
    in_specs = [_row_spec(rows)]
    args = [h]
    if grid_shift:
        in_specs += [
            pl.BlockSpec((None, GRID_W, d), lambda bi, j: (bi, jnp.maximum(j * per - 1, 0), 0)),
            pl.BlockSpec((None, GRID_W, d), lambda bi, j: (bi, jnp.minimum((j + 1) * per, nrow_blocks - 1), 0)),
        ]
        args += [h, h]
    params = [norm_g.reshape(1, d), mu, w_rkv, w0, w1, w2, a0, a1, a2, g1, g2,
              k_k.reshape(1, d), k_a.reshape(1, d)]
    in_specs += [_mod_spec(3), _mod_spec(4)] + [_const_spec(p.shape) for p in params]
    args += [mtab, mtab] + params
    if vres is not None:
        v0, v1, v2 = vres
        extra = [v0.reshape(1, d), v1, v2]
        in_specs += [_const_spec(p.shape) for p in extra] + [_row_spec(rows)]
        args += extra + [v_first]
    out = jax.ShapeDtypeStruct(h.shape, F32)
    return pl.pallas_call(
        functools.partial(_rfeat_body, grid_shift=grid_shift, vres=vres is not None),
        grid=(b, t // rows),
        in_specs=in_specs,
        out_specs=[_row_spec(rows)] * 10,
        out_shape=[out] * 10,
        compiler_params=_cparams(("parallel", "parallel")),
        name="rfeat_lat" if grid_shift else "rfeat_ctx",
    )(*args)


def _rscan_body(*refs):
    fwd, bwd = refs[0:6], refs[6:12]
    s0_ref, yf_ref, yb_ref, sfin_ref, s_ref, row_ref = refs[12:18]
    hd = RWKV_HEAD
    nsteps = yf_ref.shape[0]

    @pl.when(pl.program_id(0) == 0)
    def _():
        s_ref[...] = s0_ref[...]

    is_fwd = lax.broadcasted_iota(jnp.int32, (1, LANES), 1) < LANES // 2

    def step(t, carry):
        tb = nsteps - 1 - t
        dec, kn, vv, kk, bn, r = (jnp.where(is_fwd, f[t], g[tb]) for f, g in zip(fwd, bwd))
        row_ref[0] = dec
        row_ref[1] = kn
        row_ref[2] = kk
        row_ref[3] = bn
        row_ref[4] = dec * r
        sa = jnp.zeros((hd, LANES), F32)
        y1 = jnp.zeros((hd, LANES), F32)
        for k in range(hd):
            sk = s_ref[k]
            sa = sa + sk * row_ref[2, k:k + 1, :]
            y1 = y1 + sk * row_ref[4, k:k + 1, :]
        br = jnp.sum(bn * r, axis=0, keepdims=True)
        kr = jnp.sum(kn * r, axis=0, keepdims=True)
        y = y1 - sa * br + vv * kr
        yf_ref[t] = y
        yb_ref[tb] = y
        for k in range(hd):
            s_ref[k] = (s_ref[k] * row_ref[0, k:k + 1, :] - sa * row_ref[3, k:k + 1, :]
                        + vv * row_ref[1, k:k + 1, :])
        return carry

    lax.fori_loop(0, nsteps, step, 0)

    @pl.when(pl.program_id(0) == pl.num_programs(0) - 1)
    def _():
        sfin_ref[...] = s_ref[...]


def _rscan_call(feats, s0):
    t, hd, p = feats[0].shape
    assert p == LANES
    nblk = t // RSCAN_STEPS
    fwd = pl.BlockSpec((RSCAN_STEPS, hd, LANES), lambda i: (i, 0, 0))
    bwd = pl.BlockSpec((RSCAN_STEPS, hd, LANES), lambda i: (nblk - 1 - i, 0, 0))
    st = pl.BlockSpec((hd, hd, LANES), lambda i: (0, 0, 0))
    seq_shape = jax.ShapeDtypeStruct((t, hd, LANES), F32)
    return pl.pallas_call(
        _rscan_body,
        grid=(nblk,),
        in_specs=[fwd] * 6 + [bwd] * 6 + [st],
        out_specs=[fwd, bwd, st],
        out_shape=[seq_shape, seq_shape, jax.ShapeDtypeStruct((hd, hd, LANES), F32)],
        scratch_shapes=[pltpu.VMEM((hd, hd, LANES), F32), pltpu.VMEM((5, hd, LANES), F32)],
        compiler_params=_cparams(("arbitrary",)),
        name="rscan",
    )(*feats, *feats, s0)


def _rread_body(yf_ref, yb_ref, r_ref, kf_ref, kb_ref, v_ref, g_ref, h_ref, gt_ref,
                rk_ref, lw_ref, lb_ref, w_ref, o_ref):
    y = yf_ref[...] + yb_ref[...]
    inv = 1.0 / RWKV_HEAD
    mean = _seg64_sum(y) * inv
    yc = y - mean
    var = _seg64_sum(yc * yc) * inv
    yn = yc * lax.rsqrt(var + GN_EPS) * lw_ref[...] + lb_ref[...]
    kb = 0.5 * (kf_ref[...] + kb_ref[...])
    bonus = _seg64_sum(r_ref[...] * kb * rk_ref[...]) * v_ref[...]
    out = _dot(((yn + bonus) * g_ref[...]).astype(BF16), w_ref[...])
    o_ref[...] = h_ref[...] + gt_ref[...] * out


def _rread_call(y_f, y_b, r, k_f, k_b, v, g, h, mtab, r_k, lnx_w, lnx_b, w_o):
    b, t, d = h.shape
    rows = min(PROJ_ROWS, t)
    return pl.pallas_call(
        _rread_body,
        grid=(b, t // rows),
        in_specs=[_row_spec(rows)] * 8 + [_mod_spec(5)] + [_const_spec((1, d))] * 3 + [_const_spec((d, d))],
        out_specs=_row_spec(rows),
        out_shape=jax.ShapeDtypeStruct(h.shape, F32),
        compiler_params=_cparams(("parallel", "parallel")),
        name="rread",
    )(y_f, y_b, r, k_f, k_b, v, g, h, mtab, r_k.reshape(1, d), lnx_w.reshape(1, d), lnx_b.reshape(1, d), w_o)


def _to_lanes(fwd, bwd):
    b, t, _ = fwd.shape
    half = LANES // 2
    assert b * RWKV_HEADS <= half
    both = jnp.stack([fwd, bwd], axis=0).reshape(2, b, t, RWKV_HEADS, RWKV_HEAD)
    both = jnp.pad(both, ((0, 0), (0, half // RWKV_HEADS - b), (0, 0), (0, 0), (0, 0)))
    return both.transpose(2, 4, 0, 1, 3).reshape(t, RWKV_HEAD, LANES)


def _from_lanes(y, direction, b):
    t = y.shape[0]
    half = LANES // 2
    y = y[:, :, direction * half:(direction + 1) * half].reshape(t, RWKV_HEAD, half // RWKV_HEADS, RWKV_HEADS)
    return y[:, :, :b].transpose(2, 0, 3, 1).reshape(b, t, D_MODEL)


def kernel(x, c, ctx, c_ctx, norm_w, mod_w, mod_b, ffn_w_in, ffn_w_out, hgrn_w_in, hgrn_lb, hgrn_norm_w, hgrn_w_out, rwkv_mu, rwkv_w_rkv, rwkv_w_o, rwkv_w0, rwkv_w1, rwkv_w2, rwkv_a0, rwkv_a1, rwkv_a2, rwkv_v0, rwkv_v1, rwkv_v2, rwkv_g1, rwkv_g2, rwkv_k_k, rwkv_k_a, rwkv_r_k, rwkv_lnx_w, rwkv_lnx_b, final_norm_w):
    b, t, d = x.shape
    tc = ctx.shape[1]
    depth = norm_w.shape[0]

    p = jax.nn.softmax(hgrn_lb.astype(F32), axis=0)
    lower_bounds = jnp.cumsum(p, axis=0) - p[0]

    rows = -(-(b + 1) // 8) * 8
    c_rows = jnp.zeros((rows, d), F32).at[:b].set(c).at[b].set(c_ctx)
    m_all = _mod_call(c_rows, mod_w, mod_b)

    h, hc = x, ctx
    v_first = v_first_c = None
    zero_state = jnp.zeros((b, HGRN_HEADS, HGRN_DK, HGRN_DK), F32)
    for i in range(depth):
        last = i == depth - 1
        j = i // N_MIXERS
        mt = m_all[i, :b].reshape(b, 9, 1, d)
        mtc = jnp.broadcast_to(m_all[i, b].reshape(1, 9, 1, d), (b, 9, 1, d))
        w_in = [ffn_w_in[i, s].astype(BF16) for s in range(2)]
        w_out = [ffn_w_out[i, s].astype(BF16) for s in range(2)]

        h = _ffn_call(h, mt, 0, norm_w[i, 0], w_in[0], w_out[0], final_norm_w, False)
        hc = _ffn_call(hc, mtc, 0, norm_w[i, 0], w_in[0], w_out[0], final_norm_w, False)

        if i % N_MIXERS == 0:
            hw_in = hgrn_w_in[j].astype(BF16)
            hw_out = hgrn_w_out[j].astype(BF16)
            v_c, q_c, gs_c, kf_c, bf_c, kb_c, bb_c = _hproj_call(hc, mtc, norm_w[i, 1], hw_in, lower_bounds[j])
            v_l, q_l, gs_l, kf_l, bf_l, kb_l, bb_l = _hproj_call(h, mt, norm_w[i, 1], hw_in, lower_bounds[j])
            ocf, ocb, s_f, s_b = _hscan_call(q_c, v_c, kf_c, bf_c, kb_c, bb_c, zero_state, zero_state)
            olf, olb, _, _ = _hscan_call(q_l, v_l, kf_l, bf_l, kb_l, bb_l, s_f, s_b)
            h = _hread_call(olf, olb, gs_l, h, mt, hgrn_norm_w[j], hw_out)
            if not last:
                hc = _hread_call(ocf, ocb, gs_c, hc, mtc, hgrn_norm_w[j], hw_out)
        else:
            lp = (rwkv_mu[j], rwkv_w_rkv[j].astype(BF16), rwkv_w0[j], rwkv_w1[j].astype(BF16),
                  rwkv_w2[j].astype(BF16), rwkv_a0[j], rwkv_a1[j].astype(BF16), rwkv_a2[j].astype(BF16),
                  rwkv_g1[j].astype(BF16), rwkv_g2[j].astype(BF16), rwkv_k_k[j], rwkv_k_a[j])
            vres = None if j == 0 else (rwkv_v0[j - 1], rwkv_v1[j - 1].astype(BF16), rwkv_v2[j - 1].astype(BF16))
            fc = _rfeat_call(hc, mtc, norm_w[i, 1], lp, vres, v_first_c, False)
            fl = _rfeat_call(h, mt, norm_w[i, 1], lp, vres, v_first, True)
            r_c, g_c, vv_c, kk_c, df_c, kf_c, bf_c, db_c, kb_c, bb_c = fc
            r_l, g_l, vv_l, kk_l, df_l, kf_l, bf_l, db_l, kb_l, bb_l = fl
            if vres is None:
                v_first, v_first_c = vv_l, vv_c
            lanes_c = [_to_lanes(f, g) for f, g in
                       ((df_c, db_c), (kf_c, kb_c), (vv_c, vv_c), (kk_c, kk_c), (bf_c, bb_c), (r_c, r_c))]
            lanes_l = [_to_lanes(f, g) for f, g in
                       ((df_l, db_l), (kf_l, kb_l), (vv_l, vv_l), (kk_l, kk_l), (bf_l, bb_l), (r_l, r_l))]
            ycf, ycb, s_ctx = _rscan_call(lanes_c, jnp.zeros((RWKV_HEAD, RWKV_HEAD, LANES), F32))
            ylf, ylb, _ = _rscan_call(lanes_l, s_ctx)
            ycf, ycb = _from_lanes(ycf, 0, b), _from_lanes(ycb, 1, b)
            ylf, ylb = _from_lanes(ylf, 0, b), _from_lanes(ylb, 1, b)
            wo = rwkv_w_o[j].astype(BF16)
            rk = rwkv_r_k[j].reshape(d)
            h = _rread_call(ylf, ylb, r_l, kf_l, kb_l, vv_l, g_l, h, mt, rk, rwkv_lnx_w[j], rwkv_lnx_b[j], wo)
            if not last:
                hc = _rread_call(ycf, ycb, r_c, kf_c, kb_c, vv_c, g_c, hc, mtc, rk, rwkv_lnx_w[j], rwkv_lnx_b[j], wo)

        h = _ffn_call(h, mt, 2, norm_w[i, 2], w_in[1], w_out[1], final_norm_w, last)
        if not last:
            hc = _ffn_call(hc, mtc, 2, norm_w[i, 2], w_in[1], w_out[1], final_norm_w, False)
    return h
```

```python
import functools
import math

import jax
import jax.numpy as jnp
from jax import lax
from jax.experimental import pallas as pl
from jax.experimental.pallas import tpu as pltpu

F32 = jnp.float32
BF16 = jnp.bfloat16

D_MODEL = 1024
DEPTH = 4
N_MIXERS = 2
GRID_W = 64
D_FF = 2816
HGRN_HEADS = 8
HGRN_DK = D_MODEL // HGRN_HEADS
RWKV_HEAD = 64
RWKV_HEADS = D_MODEL // RWKV_HEAD
RMS_EPS = 1e-6
GN_EPS = 64e-5

LANES = 128
VMEM_LIMIT = 56 * 1024 * 1024
FFN_ROWS = 1024
PROJ_ROWS = 256
FFN_COLS = D_FF // 11
GLA_CHUNK = 64
GLA_SUB = 16
GLA_BLOCK = 512
HSCAN_HEADS = 2
RSCAN_STEPS = 16


def _cparams(sem):
    return pltpu.CompilerParams(dimension_semantics=sem, vmem_limit_bytes=VMEM_LIMIT)


def _const_spec(shape):
    nd = len(shape)
    return pl.BlockSpec(shape, lambda *_: (0,) * nd, pipeline_mode=pl.Buffered(1))


def _row_spec(rows):
    return pl.BlockSpec((None, rows, D_MODEL), lambda b, j: (b, j, 0))


def _mod_spec(idx):
    return pl.BlockSpec((None, None, 1, D_MODEL), lambda b, j: (b, idx, 0, 0))


def _dot(a, b):
    return jnp.dot(a, b, preferred_element_type=F32)


def _dot_nt(a, b):
    return lax.dot_general(a, b, (((1,), (1,)), ((), ())), preferred_element_type=F32)


def _dot_tn(a, b):
    return lax.dot_general(a, b, (((0,), (0,)), ((), ())), preferred_element_type=F32)


def _sigmoid(x):
    return 1.0 / (1.0 + jnp.exp(-x))


def _silu(x):
    return x * _sigmoid(x)


def _rms(x, g):
    ms = jnp.mean(x * x, axis=-1, keepdims=True)
    return x * lax.rsqrt(ms + RMS_EPS) * g


def _rms_mod(x, g, shift, scale):
    return _rms(x, g) * (1.0 + scale) + shift


def _split3(x):
    hi = x.astype(BF16)
    r1 = x - hi.astype(F32)
    mid = r1.astype(BF16)
    lo = (r1 - mid.astype(F32)).astype(BF16)
    return hi, mid, lo


def _dot_exact_lhs(m_bf16, x):
    hi, mid, lo = _split3(x)
    return _dot(m_bf16, hi) + _dot(m_bf16, mid) + _dot(m_bf16, lo)


def _seg64_sum(x):
    cols = []
    lane = lax.broadcasted_iota(jnp.int32, (1, LANES), 1)
    low = lane < RWKV_HEAD
    for i in range(x.shape[-1] // LANES):
        xt = x[:, i * LANES:(i + 1) * LANES]
        s_lo = jnp.sum(jnp.where(low, xt, 0.0), axis=-1, keepdims=True)
        s_hi = jnp.sum(jnp.where(low, 0.0, xt), axis=-1, keepdims=True)
        cols.append(jnp.where(low, s_lo, s_hi))
    return jnp.concatenate(cols, axis=-1)


def _mod_body(c_ref, w_ref, b_ref, o_ref):
    s = _silu(c_ref[...])
    o_ref[...] = jnp.dot(s, w_ref[...], preferred_element_type=F32,
                         precision=lax.Precision.HIGHEST) + b_ref[...]


def _mod_call(c_rows, mod_w, mod_b):
    depth, d, n = mod_w.shape
    rows = c_rows.shape[0]
    return pl.pallas_call(
        _mod_body,
        grid=(depth, n // d),
        in_specs=[
            pl.BlockSpec((rows, d), lambda l, j: (0, 0)),
            pl.BlockSpec((None, d, d), lambda l, j: (l, 0, j)),
            pl.BlockSpec((None, 1, d), lambda l, j: (l, 0, j)),
        ],
        out_specs=pl.BlockSpec((None, rows, d), lambda l, j: (l, 0, j)),
        out_shape=jax.ShapeDtypeStruct((depth, rows, n), F32),
        compiler_params=_cparams(("arbitrary", "arbitrary")),
        name="mod",
    )(c_rows, mod_w, mod_b.reshape(depth, 1, n))


def _ffn_body(h_ref, sh_ref, sc_ref, gt_ref, g_ref, win_ref, wout_ref, fw_ref, o_ref, *, final):
    x = h_ref[...]
    xn = _rms_mod(x, g_ref[...], sh_ref[...], sc_ref[...]).astype(BF16)
    acc = None
    for c in range(D_FF // FFN_COLS):
        lo = c * FFN_COLS
        gate = _dot(xn, win_ref[:, lo:lo + FFN_COLS])
        up = _dot(xn, win_ref[:, D_FF + lo:D_FF + lo + FFN_COLS])
        act = (_silu(gate) * up).astype(BF16)
        part = _dot(act, wout_ref[lo:lo + FFN_COLS, :])
        acc = part if acc is None else acc + part
    out = x + 0.5 * gt_ref[...] * acc
    if final:
        out = _rms(out, fw_ref[...])
    o_ref[...] = out


def _ffn_call(h, mtab, sub, norm_g, w_in, w_out, final_w, final):
    b, t, d = h.shape
    rows = min(FFN_ROWS, t)
    return pl.pallas_call(
        functools.partial(_ffn_body, final=final),
        grid=(b, t // rows),
        in_specs=[
            _row_spec(rows),
            _mod_spec(3 * sub), _mod_spec(3 * sub + 1), _mod_spec(3 * sub + 2),
            _const_spec((1, d)),
            _const_spec((d, 2 * D_FF)),
            _const_spec((D_FF, d)),
            _const_spec((1, d)),
        ],
        out_specs=_row_spec(rows),
        out_shape=jax.ShapeDtypeStruct(h.shape, F32),
        compiler_params=_cparams(("parallel", "parallel")),
        name="ffn",
    )(h, mtab, mtab, mtab, norm_g.reshape(1, d), w_in, w_out, final_w.reshape(1, d))


def _chunk_tri(rows, reverse):
    r = lax.broadcasted_iota(jnp.int32, (rows, rows), 0)
    c = lax.broadcasted_iota(jnp.int32, (rows, rows), 1)
    same = (r // GLA_CHUNK) == (c // GLA_CHUNK)
    tri = (c >= r) if reverse else (c <= r)
    return jnp.where(same & tri, 1.0, 0.0).astype(BF16)


def _hproj_body(h_ref, sh_ref, sc_ref, g_ref, w_ref, lb_ref,
                v_ref, q_ref, gs_ref, kf_ref, bf_ref, kb_ref, bb_ref):
    d = D_MODEL
    u = _rms_mod(h_ref[...], g_ref[...], sh_ref[...], sc_ref[...]).astype(BF16)
    rows = u.shape[0]
    v_ref[...] = _dot(u, w_ref[:, 0:d])
    q_ref[...] = _silu(_dot(u, w_ref[:, 3 * d:4 * d]))
    gs_ref[...] = _silu(_dot(u, w_ref[:, 4 * d:5 * d]))
    for n, (k_ref, b_ref) in enumerate(((kf_ref, bf_ref), (kb_ref, bb_ref))):
        z = _dot(u, w_ref[:, (1 + n) * d:(2 + n) * d])
        lb = lb_ref[n:n + 1, :]
        k_ref[...] = (1.0 - lb) * _sigmoid(-z)
        logf = jnp.log(lb + (1.0 - lb) * _sigmoid(z))
        b_ref[...] = _dot_exact_lhs(_chunk_tri(rows, n == 1), logf)


def _hproj_call(h, mtab, norm_g, w_in, lb):
    b, t, d = h.shape
    rows = min(PROJ_ROWS, t)
    out = jax.ShapeDtypeStruct(h.shape, F32)
    return pl.pallas_call(
        _hproj_body,
        grid=(b, t // rows),
        in_specs=[
            _row_spec(rows), _mod_spec(3), _mod_spec(4),
            _const_spec((1, d)), _const_spec((d, 5 * d)), _const_spec((2, d)),
        ],
        out_specs=[_row_spec(rows)] * 7,
        out_shape=[out] * 7,
        compiler_params=_cparams(("parallel", "parallel")),
        name="hproj",
    )(h, mtab, mtab, norm_g.reshape(1, d), w_in, lb)


def _gla_chunk(q, k, v, bc, st, reverse):
    c_, s_ = GLA_CHUNK, GLA_SUB
    nsub = c_ // s_
    dk = HGRN_DK
    row = lax.broadcasted_iota(jnp.int32, (c_, c_), 0)
    col = lax.broadcasted_iota(jnp.int32, (c_, c_), 1)
    same_sub = (row // s_) == (col // s_)
    col_in_sub = col % s_
    sub_row = lax.broadcasted_iota(jnp.int32, (1, s_, 1), 1)
    row1 = lax.broadcasted_iota(jnp.int32, (c_, 1), 0)

    b3 = bc.reshape(nsub, s_, dk)
    zero = jnp.zeros((1, 1, dk), F32)
    if reverse:
        bref3 = jnp.concatenate([b3[1:, 0:1, :], zero], axis=0)
    else:
        bref3 = jnp.concatenate([zero, b3[:-1, s_ - 1:s_, :]], axis=0)
    q3 = q.reshape(nsub, s_, dk)
    k3 = k.reshape(nsub, s_, dk)
    qs = (q3 * jnp.exp(b3 - bref3)).reshape(c_, dk).astype(BF16)

    o = _dot_nt((q * jnp.exp(bc)).astype(BF16), st.astype(BF16))

    att_rows = []
    for i in range(nsub):
        if (i == nsub - 1) if reverse else (i == 0):
            att_rows.append(jnp.zeros((s_, c_), F32))
            continue
        if reverse:
            ref_row = bc[(i + 1) * s_:(i + 1) * s_ + 1, :]
            valid = row1 >= (i + 1) * s_
        else:
            ref_row = bc[i * s_ - 1:i * s_, :]
            valid = row1 < i * s_
        kd = jnp.where(valid, k * jnp.exp(jnp.where(valid, ref_row - bc, 0.0)), 0.0)
        att_rows.append(_dot_nt(qs[i * s_:(i + 1) * s_, :], kd.astype(BF16)))
    att = jnp.concatenate(att_rows, axis=0)

    for jj in range(s_):
        keep = (sub_row <= jj) if reverse else (sub_row >= jj)
        prod = jnp.where(keep, q3 * jnp.exp(b3 - b3[:, jj:jj + 1, :]) * k3[:, jj:jj + 1, :], 0.0)
        sc = jnp.sum(prod, axis=-1, keepdims=True).reshape(c_, 1)
        att = att + jnp.where(same_sub & (col_in_sub == jj), sc, 0.0)

    o = o + _dot(att.astype(BF16), v.astype(BF16))
    btot = bc[0:1, :] if reverse else bc[c_ - 1:c_, :]
    kh = (k * jnp.exp(btot - bc)).astype(BF16)
    return o, st * jnp.exp(btot) + _dot_tn(v.astype(BF16), kh)


def _hscan_body(qf_ref, kf_ref, vf_ref, bf_ref, qb_ref, kb_ref, vb_ref, bb_ref, s0f_ref, s0b_ref,
                of_ref, ob_ref, sff_ref, sfb_ref, stf_ref, stb_ref, *, nchunk):
    c_ = GLA_CHUNK

    @pl.when(pl.program_id(2) == 0)
    def _():
        stf_ref[...] = s0f_ref[...]
        stb_ref[...] = s0b_ref[...]

    def chunk(ci, carry):
        off = pl.multiple_of(ci * c_, c_)
        offb = pl.multiple_of((nchunk - 1 - ci) * c_, c_)
        for hd in range(HSCAN_HEADS):
            cols = slice(hd * HGRN_DK, (hd + 1) * HGRN_DK)
            o, st = _gla_chunk(qf_ref[pl.ds(off, c_), cols], kf_ref[pl.ds(off, c_), cols],
                               vf_ref[pl.ds(off, c_), cols], bf_ref[pl.ds(off, c_), cols], stf_ref[hd], False)
            of_ref[pl.ds(off, c_), cols] = o
            stf_ref[hd] = st
            o, st = _gla_chunk(qb_ref[pl.ds(offb, c_), cols], kb_ref[pl.ds(offb, c_), cols],
                               vb_ref[pl.ds(offb, c_), cols], bb_ref[pl.ds(offb, c_), cols], stb_ref[hd], True)
            ob_ref[pl.ds(offb, c_), cols] = o
            stb_ref[hd] = st
        return carry

    lax.fori_loop(0, nchunk, chunk, 0)
    sff_ref[...] = stf_ref[...]
    sfb_ref[...] = stb_ref[...]


def _hscan_call(q, v, k_f, bc_f, k_b, bc_b, s0_f, s0_b):
    b, t, d = q.shape
    tb = min(GLA_BLOCK, t)
    nblk = t // tb
    dk = HGRN_DK
    nh = HSCAN_HEADS
    fwd = pl.BlockSpec((None, tb, nh * dk), lambda bi, hi, j: (bi, j, hi))
    bwd = pl.BlockSpec((None, tb, nh * dk), lambda bi, hi, j: (bi, nblk - 1 - j, hi))
    st = pl.BlockSpec((None, nh, dk, dk), lambda bi, hi, j: (bi, hi, 0, 0))
    seq_shape = jax.ShapeDtypeStruct(q.shape, F32)
    st_shape = jax.ShapeDtypeStruct((b, HGRN_HEADS, dk, dk), F32)
    return pl.pallas_call(
        functools.partial(_hscan_body, nchunk=tb // GLA_CHUNK),
        grid=(b, HGRN_HEADS // nh, nblk),
        in_specs=[fwd] * 4 + [bwd] * 4 + [st, st],
        out_specs=[fwd, bwd, st, st],
        out_shape=[seq_shape, seq_shape, st_shape, st_shape],
        scratch_shapes=[pltpu.VMEM((nh, dk, dk), F32), pltpu.VMEM((nh, dk, dk), F32)],
        compiler_params=_cparams(("parallel", "parallel", "arbitrary")),
        name="hscan",
    )(q, k_f, v, bc_f, q, k_b, v, bc_b, s0_f, s0_b)


def _hread_body(of_ref, ob_ref, gs_ref, h_ref, gt_ref, ng_ref, w_ref, o_ref):
    o = of_ref[...] + ob_ref[...]
    cols = []
    for hd in range(HGRN_HEADS):
        oh = o[:, hd * HGRN_DK:(hd + 1) * HGRN_DK]
        cols.append(oh * lax.rsqrt(jnp.mean(oh * oh, axis=-1, keepdims=True) + RMS_EPS))
    on = jnp.concatenate(cols, axis=-1) * ng_ref[...]
    y = _dot((on * gs_ref[...]).astype(BF16), w_ref[...])
    o_ref[...] = h_ref[...] + gt_ref[...] * y


def _hread_call(o_f, o_b, gs, h, mtab, norm_g, w_out):
    b, t, d = h.shape
    rows = min(PROJ_ROWS, t)
    return pl.pallas_call(
        _hread_body,
        grid=(b, t // rows),
        in_specs=[_row_spec(rows)] * 4 + [_mod_spec(5), _const_spec((1, d)), _const_spec((d, d))],
        out_specs=_row_spec(rows),
        out_shape=jax.ShapeDtypeStruct(h.shape, F32),
        compiler_params=_cparams(("parallel", "parallel")),
        name="hread",
    )(o_f, o_b, gs, h, mtab, norm_g.reshape(1, d), w_out)


def _rfeat_body(*refs, grid_shift, vres):
    it = iter(refs)
    h_ref = next(it)
    hp_ref = next(it) if grid_shift else None
    hn_ref = next(it) if grid_shift else None
    sh_ref, sc_ref, g_ref, mu_ref, wrkv_ref = (next(it) for _ in range(5))
    w0_ref, w1_ref, w2_ref, a0_ref, a1_ref, a2_ref = (next(it) for _ in range(6))
    g1_ref, g2_ref, kk_ref_, ka_ref = (next(it) for _ in range(4))
    if vres:
        v0_ref, v1_ref, v2_ref, vf_ref = (next(it) for _ in range(4))
    (r_out, g_out, v_out, kk_out, df_out, kf_out, bf_out, db_out, kb_out, bb_out) = (next(it) for _ in range(10))

    d = D_MODEL
    g, sh, sc = g_ref[...], sh_ref[...], sc_ref[...]
    u = _rms_mod(h_ref[...], g, sh, sc)
    rows = u.shape[0]
    rid = lax.broadcasted_iota(jnp.int32, (rows, 1), 0)
    if grid_shift:
        j = pl.program_id(1)
        last = pl.num_programs(1) - 1
        up_rows = jnp.where(j > 0, _rms_mod(hp_ref[...], g, sh, sc), 0.0)
        dn_rows = jnp.where(j < last, _rms_mod(hn_ref[...], g, sh, sc), 0.0)
        q4 = d // 4
        col = rid % GRID_W
        left = jnp.where(col != 0, pltpu.roll(u[:, 0:q4], 1, 0), 0.0)
        right = jnp.where(col != GRID_W - 1, pltpu.roll(u[:, q4:2 * q4], rows - 1, 0), 0.0)
        up = jnp.concatenate([up_rows[:, 2 * q4:3 * q4], u[:rows - GRID_W, 2 * q4:3 * q4]], axis=0)
        down = jnp.concatenate([u[GRID_W:, 3 * q4:], dn_rows[:, 3 * q4:]], axis=0)
        xs = jnp.concatenate([left, right, up, down], axis=-1)
    else:
        h2 = d // 2
        prev = jnp.where(rid != 0, pltpu.roll(u[:, 0:h2], 1, 0), 0.0)
        nxt = jnp.where(rid != rows - 1, pltpu.roll(u[:, h2:], rows - 1, 0), 0.0)
        xs = jnp.concatenate([prev, nxt], axis=-1)
    xx = xs - u

    def mix(n):
        return (u + xx * mu_ref[n:n + 1, :]).astype(BF16)

    xv = mix(3)
    k = _dot(mix(2), wrkv_ref[1])
    v = _dot(xv, wrkv_ref[2])
    if vres:
        lora = _dot(_dot(xv, v1_ref[...]).astype(BF16), v2_ref[...])
        v = v + (vf_ref[...] - v) * _sigmoid(v0_ref[...] + lora)
    kk = k * kk_ref_[...]
    kk = kk * lax.rsqrt(jnp.maximum(_seg64_sum(kk * kk), 1e-24))
    r_out[...] = _dot(mix(0), wrkv_ref[0])
    g_out[...] = _dot(_sigmoid(_dot(mix(5), g1_ref[...])).astype(BF16), g2_ref[...])
    v_out[...] = v
    kk_out[...] = kk
    xw = mix(1)
    xa = mix(4)
    ka = ka_ref[...]
    scale = math.exp(-0.5)
    for n, (d_out, k_out, b_out) in enumerate(((df_out, kf_out, bf_out), (db_out, kb_out, bb_out))):
        wpre = w0_ref[n:n + 1, :] + _dot(jnp.tanh(_dot(xw, w1_ref[n])).astype(BF16), w2_ref[n])
        d_out[...] = jnp.exp(-(_sigmoid(wpre) * scale))
        a = _sigmoid(a0_ref[n:n + 1, :] + _dot(_dot(xa, a1_ref[n]).astype(BF16), a2_ref[n]))
        k_out[...] = k * (1.0 + (a - 1.0) * ka)
        b_out[...] = kk * a


def _rfeat_call(h, mtab, norm_g, lp, vres, v_first, grid_shift):
    b, t, d = h.shape
    rows = min(PROJ_ROWS, t) if grid_shift else t
    mu, w_rkv, w0, w1, w2, a0, a1, a2, g1, g2, k_k, k_a = lp
    nrow_blocks = t // GRID_W
    per = rows // GRID_W
    in_specs = [_row_spec(rows)]
    args = [h]
    if grid_shift:
        in_specs += [
            pl.BlockSpec((None, GRID_W, d), lambda bi, j: (bi, jnp.maximum(j * per - 1, 0), 0)),
            pl.BlockSpec((None, GRID_W, d), lambda bi, j: (bi, jnp.minimum((j + 1) * per, nrow_blocks - 1), 0)),
        ]
        args += [h, h]
    params = [norm_g.reshape(1, d), mu, w_rkv, w0, w1, w2, a0, a1, a2, g1, g2,
              k_k.reshape(1, d), k_a.reshape(1, d)]
    in_specs += [_mod_spec(3), _mod_spec(4)] + [_const_spec(p.shape) for p in params]
    args += [mtab, mtab] + params
    if vres is not None:
        v0, v1, v2 = vres
        extra = [v0.reshape(1, d), v1, v2]
        in_specs += [_const_spec(p.shape) for p in extra] + [_row_spec(rows)]
        args += extra + [v_first]
    out = jax.ShapeDtypeStruct(h.shape, F32)
    return pl.pallas_call(
        functools.partial(_rfeat_body, grid_shift=grid_shift, vres=vres is not None),
        grid=(b, t // rows),
        in_specs=in_specs,
        out_specs=[_row_spec(rows)] * 10,
        out_shape=[out] * 10,
        compiler_params=_cparams(("parallel", "parallel")),
        name="rfeat_lat" if grid_shift else "rfeat_ctx",
    )(*args)


def _rscan_body(*refs):
    fwd, bwd = refs[0:6], refs[6:12]
    s0_ref, yf_ref, yb_ref, sfin_ref, s_ref, row_ref = refs[12:18]
    hd = RWKV_HEAD
    nsteps = yf_ref.shape[0]

    @pl.when(pl.program_id(0) == 0)
    def _():
        s_ref[...] = s0_ref[...]

    is_fwd = lax.broadcasted_iota(jnp.int32, (1, LANES), 1) < LANES // 2

    def step(t, carry):
        tb = nsteps - 1 - t
        dec, kn, vv, kk, bn, r = (jnp.where(is_fwd, f[t], g[tb]) for f, g in zip(fwd, bwd))
        row_ref[0] = dec
        row_ref[1] = kn
        row_ref[2] = kk
        row_ref[3] = bn
        row_ref[4] = dec * r
        sa = jnp.zeros((hd, LANES), F32)
        y1 = jnp.zeros((hd, LANES), F32)
        for k in range(hd):
            sk = s_ref[k]
            sa = sa + sk * row_ref[2, k:k + 1, :]
            y1 = y1 + sk * row_ref[4, k:k + 1, :]
        br = jnp.sum(bn * r, axis=0, keepdims=True)
        kr = jnp.sum(kn * r, axis=0, keepdims=True)
        y = y1 - sa * br + vv * kr
        yf_ref[t] = y
        yb_ref[tb] = y
        for k in range(hd):
            s_ref[k] = (s_ref[k] * row_ref[0, k:k + 1, :] - sa * row_ref[3, k:k + 1, :]
                        + vv * row_ref[1, k:k + 1, :])
        return carry

    lax.fori_loop(0, nsteps, step, 0)

    @pl.when(pl.program_id(0) == pl.num_programs(0) - 1)
    def _():
        sfin_ref[...] = s_ref[...]


def _rscan_call(feats, s0):
    t, hd, p = feats[0].shape
    assert p == LANES
    nblk = t // RSCAN_STEPS
    fwd = pl.BlockSpec((RSCAN_STEPS, hd, LANES), lambda i: (i, 0, 0))
    bwd = pl.BlockSpec((RSCAN_STEPS, hd, LANES), lambda i: (nblk - 1 - i, 0, 0))
    st = pl.BlockSpec((hd, hd, LANES), lambda i: (0, 0, 0))
    seq_shape = jax.ShapeDtypeStruct((t, hd, LANES), F32)
    return pl.pallas_call(
        _rscan_body,
        grid=(nblk,),
        in_specs=[fwd] * 6 + [bwd] * 6 + [st],
        out_specs=[fwd, bwd, st],
        out_shape=[seq_shape, seq_shape, jax.ShapeDtypeStruct((hd, hd, LANES), F32)],
        scratch_shapes=[pltpu.VMEM((hd, hd, LANES), F32), pltpu.VMEM((5, hd, LANES), F32)],
        compiler_params=_cparams(("arbitrary",)),
        name="rscan",
    )(*feats, *feats, s0)


def _rread_body(yf_ref, yb_ref, r_ref, kf_ref, kb_ref, v_ref, g_ref, h_ref, gt_ref,
                rk_ref, lw_ref, lb_ref, w_ref, o_ref):
    y = yf_ref[...] + yb_ref[...]
    inv = 1.0 / RWKV_HEAD
    mean = _seg64_sum(y) * inv
    yc = y - mean
    var = _seg64_sum(yc * yc) * inv
    yn = yc * lax.rsqrt(var + GN_EPS) * lw_ref[...] + lb_ref[...]
    kb = 0.5 * (kf_ref[...] + kb_ref[...])
    bonus = _seg64_sum(r_ref[...] * kb * rk_ref[...]) * v_ref[...]
    out = _dot(((yn + bonus) * g_ref[...]).astype(BF16), w_ref[...])
    o_ref[...] = h_ref[...] + gt_ref[...] * out


def _rread_call(y_f, y_b, r, k_f, k_b, v, g, h, mtab, r_k, lnx_w, lnx_b, w_o):
    b, t, d = h.shape
    rows = min(PROJ_ROWS, t)
    return pl.pallas_call(
        _rread_body,
        grid=(b, t // rows),
        in_specs=[_row_spec(rows)] * 8 + [_mod_spec(5)] + [_const_spec((1, d))] * 3 + [_const_spec((d, d))],
        out_specs=_row_spec(rows),
        out_shape=jax.ShapeDtypeStruct(h.shape, F32),
        compiler_params=_cparams(("parallel", "parallel")),
        name="rread",
    )(y_f, y_b, r, k_f, k_b, v, g, h, mtab, r_k.reshape(1, d), lnx_w.reshape(1, d), lnx_b.reshape(1, d), w_o)


def _to_lanes(fwd, bwd):
    b, t, _ = fwd.shape
    half = LANES // 2
    assert b * RWKV_HEADS <= half
    both = jnp.stack([fwd, bwd], axis=0).reshape(2, b, t, RWKV_HEADS, RWKV_HEAD)
    both = jnp.pad(both, ((0, 0), (0, half // RWKV_HEADS - b), (0, 0), (0, 0), (0, 0)))
    return both.transpose(2, 4, 0, 1, 3).reshape(t, RWKV_HEAD, LANES)


def _from_lanes(y, direction, b):
    t = y.shape[0]
    half = LANES // 2
    y = y[:, :, direction * half:(direction + 1) * half].reshape(t, RWKV_HEAD, half // RWKV_HEADS, RWKV_HEADS)
    return y[:, :, :b].transpose(2, 0, 3, 1).reshape(b, t, D_MODEL)


def kernel(x, c, ctx, c_ctx, norm_w, mod_w, mod_b, ffn_w_in, ffn_w_out, hgrn_w_in, hgrn_lb, hgrn_norm_w, hgrn_w_out, rwkv_mu, rwkv_w_rkv, rwkv_w_o, rwkv_w0, rwkv_w1, rwkv_w2, rwkv_a0, rwkv_a1, rwkv_a2, rwkv_v0, rwkv_v1, rwkv_v2, rwkv_g1, rwkv_g2, rwkv_k_k, rwkv_k_a, rwkv_r_k, rwkv_lnx_w, rwkv_lnx_b, final_norm_w):
    b, t, d = x.shape
    tc = ctx.shape[1]
    depth = norm_w.shape[0]

    p = jax.nn.softmax(hgrn_lb.astype(F32), axis=0)
    lower_bounds = jnp.cumsum(p, axis=0) - p[0]

    rows = -(-(b + 1) // 8) * 8
    c_rows = jnp.zeros((rows, d), F32).at[:b].set(c).at[b].set(c_ctx)
    m_all = _mod_call(c_rows, mod_w, mod_b)

    h, hc = x, ctx
    v_first = v_first_c = None
    zero_state = jnp.zeros((b, HGRN_HEADS, HGRN_DK, HGRN_DK), F32)
    for i in range(depth):
        last = i == depth - 1
        j = i // N_MIXERS
        mt = m_all[i, :b].reshape(b, 9, 1, d)
        mtc = jnp.broadcast_to(m_all[i, b].reshape(1, 9, 1, d), (b, 9, 1, d))
        w_in = [ffn_w_in[i, s].astype(BF16) for s in range(2)]
        w_out = [ffn_w_out[i, s].astype(BF16) for s in range(2)]

        h = _ffn_call(h, mt, 0, norm_w[i, 0], w_in[0], w_out[0], final_norm_w, False)
        hc = _ffn_call(hc, mtc, 0, norm_w[i, 0], w_in[0], w_out[0], final_norm_w, False)

        if i % N_MIXERS == 0:
            hw_in = hgrn_w_in[j].astype(BF16)
            hw_out = hgrn_w_out[j].astype(BF16)
            v_c, q_c, gs_c, kf_c, bf_c, kb_c, bb_c = _hproj_call(hc, mtc, norm_w[i, 1], hw_in, lower_bounds[j])
            v_l, q_l, gs_l, kf_l, bf_l, kb_l, bb_l = _hproj_call(h, mt, norm_w[i, 1], hw_in, lower_bounds[j])
            ocf, ocb, s_f, s_b = _hscan_call(q_c, v_c, kf_c, bf_c, kb_c, bb_c, zero_state, zero_state)
            olf, olb, _, _ = _hscan_call(q_l, v_l, kf_l, bf_l, kb_l, bb_l, s_f, s_b)
            h = _hread_call(olf, olb, gs_l, h, mt, hgrn_norm_w[j], hw_out)
            if not last:
                hc = _hread_call(ocf, ocb, gs_c, hc, mtc, hgrn_norm_w[j], hw_out)
        else:
            lp = (rwkv_mu[j], rwkv_w_rkv[j].astype(BF16), rwkv_w0[j], rwkv_w1[j].astype(BF16),
                  rwkv_w2[j].astype(BF16), rwkv_a0[j], rwkv_a1[j].astype(BF16), rwkv_a2[j].astype(BF16),
                  rwkv_g1[j].astype(BF16), rwkv_g2[j].astype(BF16), rwkv_k_k[j], rwkv_k_a[j])
            vres = None if j == 0 else (rwkv_v0[j - 1], rwkv_v1[j - 1].astype(BF16), rwkv_v2[j - 1].astype(BF16))
            fc = _rfeat_call(hc, mtc, norm_w[i, 1], lp, vres, v_first_c, False)
            fl = _rfeat_call(h, mt, norm_w[i, 1], lp, vres, v_first, True)
            r_c, g_c, vv_c, kk_c, df_c, kf_c, bf_c, db_c, kb_c, bb_c = fc
            r_l, g_l, vv_l, kk_l, df_l, kf_l, bf_l, db_l, kb_l, bb_l = fl
            if vres is None:
                v_first, v_first_c = vv_l, vv_c
            lanes_c = [_to_lanes(f, g) for f, g in
                       ((df_c, db_c), (kf_c, kb_c), (vv_c, vv_c), (kk_c, kk_c), (bf_c, bb_c), (r_c, r_c))]
            lanes_l = [_to_lanes(f, g) for f, g in
                       ((df_l, db_l), (kf_l, kb_l), (vv_l, vv_l), (kk_l, kk_l), (bf_l, bb_l), (r_l, r_l))]
            ycf, ycb, s_ctx = _rscan_call(lanes_c, jnp.zeros((RWKV_HEAD, RWKV_HEAD, LANES), F32))
            ylf, ylb, _ = _rscan_call(lanes_l, s_ctx)
            ycf, ycb = _from_lanes(ycf, 0, b), _from_lanes(ycb, 1, b)
            ylf, ylb = _from_lanes(ylf, 0, b), _from_lanes(ylb, 1, b)
            wo = rwkv_w_o[j].astype(BF16)
            rk = rwkv_r_k[j].reshape(d)
            h = _rread_call(ylf, ylb, r_l, kf_l, kb_l, vv_l, g_l, h, mt, rk, rwkv_lnx_w[j], rwkv_lnx_b[j], wo)
            if not last:
                hc = _rread_call(ycf, ycb, r_c, kf_c, kb_c, vv_c, g_c, hc, mtc, rk, rwkv_lnx_w[j], rwkv_lnx_b[j], wo)

        h = _ffn_call(h, mt, 2, norm_w[i, 2], w_in[1], w_out[1], final_norm_w, last)
        if not last:
            hc = _ffn_call(hc, mtc, 2, norm_w[i, 2], w_in[1], w_out[1], final_norm_w, False)
    return h
```
